```python
import jax, jax.numpy as jnp
from jax import lax
import numpy as np

D_MODEL = 4096
BATCH = 16
SEQ = 256
DEPTH = 4
DEC_BATCH = 2
DEC_SEQ = 4096
PAST_LEN = 512

GRID_W = 64
H_A = 8
Q_LORA = 768
KV_LORA = 512
NOPE_A = 128
ROPE_A = 64
V_A = 128
H_B = 16
KV_B = 4
HD_B = 128
G_B = H_B // KV_B
C_CONV = D_MODEL - H_A * V_A - H_B * HD_B
CONV_W = 31
ROPE_THETA = 10000.0
Q_BLOCK = 128
MLA_SCALE = (NOPE_A + ROPE_A) ** -0.5
GQA_SCALE = HD_B ** -0.5
N_EXPERTS = 16
N_GROUPS = 4
EXPERTS_PER_GROUP = N_EXPERTS // N_GROUPS
TOP_K = 2
D_EXPERT = 1024
EPS = 1e-6
IN_WIDTHS = (Q_LORA, KV_LORA, ROPE_A, H_B * HD_B, KV_B * HD_B, KV_B * HD_B, 2 * C_CONV)
D_IN = sum(IN_WIDTHS)
N_MOD = 6

kernel_name = 'hybrid_mla_gqa_conformer_moe_dit_step'


def rms_norm(x, g):
    xf = x.astype(jnp.float32)
    y = xf * lax.rsqrt(jnp.mean(xf * xf, axis=-1, keepdims=True) + EPS)
    return (y * g.astype(jnp.float32)).astype(x.dtype)


def layer_norm(x, g, b):
    xf = x.astype(jnp.float32)
    mu = jnp.mean(xf, axis=-1, keepdims=True)
    var = jnp.mean(jnp.square(xf - mu), axis=-1, keepdims=True)
    y = (xf - mu) * lax.rsqrt(var + EPS)
    return (y * g.astype(jnp.float32) + b.astype(jnp.float32)).astype(x.dtype)


def axial_rope_tables(n_tokens, rot_dim):
    rows = n_tokens // GRID_W
    row = jnp.broadcast_to(jnp.arange(rows, dtype=jnp.float32)[:, None], (rows, GRID_W)).reshape(n_tokens)
    col = jnp.broadcast_to(jnp.arange(GRID_W, dtype=jnp.float32)[None, :], (rows, GRID_W)).reshape(n_tokens)
    quarter = rot_dim // 4
    inv_freq = ROPE_THETA ** (-jnp.arange(quarter, dtype=jnp.float32) / quarter)
    ang = jnp.concatenate([row[:, None] * inv_freq, col[:, None] * inv_freq], axis=-1)
    return jnp.cos(ang), jnp.sin(ang)


def apply_rope(x, cos, sin):
    xf = x.astype(jnp.float32)
    half = xf.shape[-1] // 2
    x1, x2 = xf[..., :half], xf[..., half:]
    cs, sn = cos[None, :, None, :], sin[None, :, None, :]
    return jnp.concatenate([x1 * cs - x2 * sn, x1 * sn + x2 * cs], axis=-1).astype(x.dtype)


def attend(q, k, v, scale):
    b, tq, hk, g, dk = q.shape
    nb = tq // Q_BLOCK
    qb = q.reshape(b, nb, Q_BLOCK, hk, g, dk).transpose(1, 0, 2, 3, 4, 5)

    def one_block(qblk):
        s = jnp.einsum('bqhgd,bkhd->bhgqk', qblk, k).astype(jnp.float32) * scale
        p = jax.nn.softmax(s, axis=-1).astype(v.dtype)
        return jnp.einsum('bhgqk,bkhd->bqhgd', p, v)

    out = lax.map(one_block, qb)
    return out.transpose(1, 0, 2, 3, 4, 5).reshape(b, tq, hk * g * v.shape[-1])


def split_columns(p):
    idx, acc = [], 0
    for w in IN_WIDTHS[:-1]:
        acc += w
        idx.append(acc)
    return jnp.split(p, idx, axis=-1)


def conv_module(u, w_dw, b_dw, g_ln, b_ln):
    a, gate = jnp.split(u, 2, axis=-1)
    v = a * jax.nn.sigmoid(gate)
    y = lax.conv_general_dilated(v, w_dw[:, None, :], window_strides=(1,),
                                 padding=[(CONV_W // 2, CONV_W // 2)],
                                 dimension_numbers=('NWC', 'WIO', 'NWC'),
                                 feature_group_count=C_CONV)
    y = y + b_dw
    return jax.nn.silu(layer_norm(y, g_ln, b_ln))


def mixing(h, lp, ctx, rope_a, rope_b):
    b, t, _ = h.shape
    p = h @ lp['w_in']
    q_lat, kv_lat, k_pe, q_b, k_b, v_b, u_c = split_columns(p)
    q_a = (rms_norm(q_lat, lp['g_q_lat']) @ lp['w_uq']).reshape(b, t, H_A, NOPE_A + ROPE_A)
    q_nope, q_pe = q_a[..., :NOPE_A], q_a[..., NOPE_A:]
    ckv = rms_norm(kv_lat, lp['g_kv_lat'])
    q_b = rms_norm(q_b.reshape(b, t, H_B, HD_B), lp['g_q_b'])
    k_b = rms_norm(k_b.reshape(b, t, KV_B, HD_B), lp['g_k_b'])
    v_b = v_b.reshape(b, t, KV_B, HD_B)
    if rope_a is not None:
        q_pe = apply_rope(q_pe, *rope_a)
        k_pe = apply_rope(k_pe[:, :, None, :], *rope_a)[:, :, 0]
        q_b = apply_rope(q_b, *rope_b)
        k_b = apply_rope(k_b, *rope_b)
    own = (ckv, k_pe, k_b, v_b)
    if ctx is None:
        ckv_all, kpe_all, kb_all, vb_all = own
    else:
        ckv_all = jnp.concatenate([ctx[0], ckv], axis=1)
        kpe_all = jnp.concatenate([ctx[1], k_pe], axis=1)
        kb_all = jnp.concatenate([ctx[2], k_b], axis=1)
        vb_all = jnp.concatenate([ctx[3], v_b], axis=1)
    tk = ckv_all.shape[1]
    kv = (ckv_all @ lp['w_ukv']).reshape(b, tk, H_A, NOPE_A + V_A)
    k_a = jnp.concatenate([kv[..., :NOPE_A],
                           jnp.broadcast_to(kpe_all[:, :, None, :], (b, tk, H_A, ROPE_A))], axis=-1)
    v_a = kv[..., NOPE_A:]
    q_full = jnp.concatenate([q_nope, q_pe], axis=-1)[:, :, :, None, :]
    out_a = attend(q_full, k_a, v_a, MLA_SCALE)
    out_b = attend(q_b.reshape(b, t, KV_B, G_B, HD_B), kb_all, vb_all, GQA_SCALE)
    out_c = conv_module(u_c, lp['w_dw'], lp['b_dw'], lp['g_conv_ln'], lp['b_conv_ln'])
    mix = jnp.concatenate([out_a, out_b, out_c], axis=-1)
    return mix @ lp['w_out'], own


def moe(h, w_router, router_bias, w_gate, w_up, w_down):
    b, t, d = h.shape
    hf = h.reshape(b * t, d)
    scores = jax.nn.sigmoid((hf @ w_router).astype(jnp.float32))
    biased = scores + router_bias.astype(jnp.float32)
    grp = biased.reshape(-1, N_GROUPS, EXPERTS_PER_GROUP)
    grp_score = jnp.sum(lax.top_k(grp, 2)[0], axis=-1)
    sel = jnp.argmax(grp_score, axis=-1)
    in_group = (jnp.arange(N_EXPERTS) // EXPERTS_PER_GROUP)[None, :] == sel[:, None]
    _, idx = lax.top_k(jnp.where(in_group, biased, -jnp.inf), TOP_K)
    w = jnp.take_along_axis(scores, idx, axis=-1)
    w = w / jnp.sum(w, axis=-1, keepdims=True)
    combine = jnp.sum(jax.nn.one_hot(idx, N_EXPERTS, dtype=jnp.float32) * w[..., None], axis=1)
    out = jnp.zeros(hf.shape, jnp.float32)
    for e in range(N_EXPERTS):
        y = (jax.nn.silu(hf @ w_gate[e]) * (hf @ w_up[e])) @ w_down[e]
        out = out + combine[:, e:e + 1] * y.astype(jnp.float32)
    return out.astype(h.dtype).reshape(b, t, d)


def block(x, m, lp, ctx, rope_a, rope_b, w_router, router_bias):
    mod = jax.nn.silu(m) @ lp['w_ada'] + lp['b_ada']
    sh1, sc1, g1, sh2, sc2, g2 = [z[:, None, :] for z in jnp.split(mod, N_MOD, axis=-1)]
    h = rms_norm(x, lp['g_norm1']) * (1 + sc1) + sh1
    mix, own = mixing(h, lp, ctx, rope_a, rope_b)
    x = x + g1 * mix
    h = rms_norm(x, lp['g_norm2']) * (1 + sc2) + sh2
    x = x + g2 * moe(h, w_router, router_bias, lp['w_gate'], lp['w_up'], lp['w_down'])
    return x, own


def setup_inputs(seed: int = 0) -> dict:
    key = jax.random.key(seed)
    ks = jax.random.split(key, 32)

    def nrm(k, shape, scale):
        return jax.random.normal(k, shape, jnp.float32) * scale

    def gain(k, shape):
        return 1.0 + 0.02 * jax.random.normal(k, shape, jnp.float32)

    return {
        'x_prompt': nrm(ks[0], (BATCH, SEQ, D_MODEL), 1.0),
        'x_sample': nrm(ks[1], (DEC_BATCH, DEC_SEQ, D_MODEL), 1.0),
        'c': nrm(ks[2], (DEC_BATCH, D_MODEL), 1.0),
        'c_ctx': nrm(ks[3], (D_MODEL,), 1.0),
        'cache_mla_ckv': nrm(ks[4], (DEC_BATCH, DEPTH, PAST_LEN, KV_LORA), 1.0),
        'cache_mla_krope': nrm(ks[5], (DEC_BATCH, DEPTH, PAST_LEN, ROPE_A), 1.0),
        'cache_gqa_k': nrm(ks[6], (DEC_BATCH, DEPTH, PAST_LEN, KV_B, HD_B), 1.0),
        'cache_gqa_v': nrm(ks[7], (DEC_BATCH, DEPTH, PAST_LEN, KV_B, HD_B), 1.0),
        'w_ada': nrm(ks[8], (DEPTH, D_MODEL, N_MOD * D_MODEL), 0.5 * D_MODEL ** -0.5),
        'b_ada': nrm(ks[9], (DEPTH, N_MOD * D_MODEL), 0.02),
        'g_norm1': gain(ks[10], (DEPTH, D_MODEL)),
        'g_norm2': gain(ks[11], (DEPTH, D_MODEL)),
        'w_in': nrm(ks[12], (DEPTH, D_MODEL, D_IN), D_MODEL ** -0.5),
        'g_q_lat': gain(ks[13], (DEPTH, Q_LORA)),
        'g_kv_lat': gain(ks[14], (DEPTH, KV_LORA)),
        'w_uq': nrm(ks[15], (DEPTH, Q_LORA, H_A * (NOPE_A + ROPE_A)), Q_LORA ** -0.5),
        'w_ukv': nrm(ks[16], (DEPTH, KV_LORA, H_A * (NOPE_A + V_A)), KV_LORA ** -0.5),
        'g_q_b': gain(ks[17], (DEPTH, HD_B)),
        'g_k_b': gain(ks[18], (DEPTH, HD_B)),
        'w_dw': nrm(ks[19], (DEPTH, CONV_W, C_CONV), CONV_W ** -0.5),
        'b_dw': nrm(ks[20], (DEPTH, C_CONV), 0.02),
        'g_conv_ln': gain(ks[21], (DEPTH, C_CONV)),
        'b_conv_ln': nrm(ks[22], (DEPTH, C_CONV), 0.02),
        'w_out': nrm(ks[23], (DEPTH, D_MODEL, D_MODEL), D_MODEL ** -0.5),
        'w_router': nrm(ks[24], (D_MODEL, N_EXPERTS), D_MODEL ** -0.5),
        'router_bias': nrm(ks[25], (N_EXPERTS,), 0.01),
        'w_gate': nrm(ks[26], (DEPTH, N_EXPERTS, D_MODEL, D_EXPERT), D_MODEL ** -0.5),
        'w_up': nrm(ks[27], (DEPTH, N_EXPERTS, D_MODEL, D_EXPERT), D_MODEL ** -0.5),
        'w_down': nrm(ks[28], (DEPTH, N_EXPERTS, D_EXPERT, D_MODEL), D_EXPERT ** -0.5),
        'g_final': gain(ks[29], (D_MODEL,)),
    }


def reference(x_prompt, x_sample, c, c_ctx, cache_mla_ckv, cache_mla_krope, cache_gqa_k, cache_gqa_v,
              w_ada, b_ada, g_norm1, g_norm2, w_in, g_q_lat, g_kv_lat, w_uq, w_ukv, g_q_b, g_k_b,
              w_dw, b_dw, g_conv_ln, b_conv_ln, w_out, w_router, router_bias, w_gate, w_up, w_down,
              g_final):
    n_lat = x_sample.shape[1]
    rope_a = axial_rope_tables(n_lat, ROPE_A)
    rope_b = axial_rope_tables(n_lat, HD_B)
    m_ctx = c_ctx[None, :]
    xp, xs = x_prompt, x_sample
    ckvs, kpes, kbs, vbs = [], [], [], []
    for l in range(DEPTH):
        lp = {
            'w_ada': w_ada[l], 'b_ada': b_ada[l], 'g_norm1': g_norm1[l], 'g_norm2': g_norm2[l],
            'w_in': w_in[l], 'g_q_lat': g_q_lat[l], 'g_kv_lat': g_kv_lat[l], 'w_uq': w_uq[l],
            'w_ukv': w_ukv[l], 'g_q_b': g_q_b[l], 'g_k_b': g_k_b[l], 'w_dw': w_dw[l], 'b_dw': b_dw[l],
            'g_conv_ln': g_conv_ln[l], 'b_conv_ln': b_conv_ln[l], 'w_out': w_out[l],
            'w_gate': w_gate[l], 'w_up': w_up[l], 'w_down': w_down[l],
        }
        xp, own = block(xp, m_ctx, lp, None, None, None, w_router, router_bias)
        ckvs.append(own[0])
        kpes.append(own[1])
        kbs.append(own[2])
        vbs.append(own[3])
        ctx = (cache_mla_ckv[:, l], cache_mla_krope[:, l], cache_gqa_k[:, l], cache_gqa_v[:, l])
        xs, _ = block(xs, c, lp, ctx, rope_a, rope_b, w_router, router_bias)
    y_prompt = rms_norm(xp, g_final)
    y_sample = rms_norm(xs, g_final)
    new_mla_ckv = jnp.stack(ckvs, axis=1)
    new_mla_krope = jnp.stack(kpes, axis=1)
    new_gqa_k = jnp.stack(kbs, axis=1)
    new_gqa_v = jnp.stack(vbs, axis=1)
    return (y_prompt, y_sample, new_mla_ckv, new_mla_krope, new_gqa_k, new_gqa_v)
```

```python
import functools
from typing import NamedTuple

import jax
import jax.numpy as jnp
from jax import lax
from jax.experimental import pallas as pl
from jax.experimental.pallas import tpu as pltpu

_F32 = jnp.float32
_BF16 = jnp.bfloat16

_GRID_W = 64
_N_GROUPS = 4
_ROPE_THETA = 10000.0
_EPS = 1e-6

_V7X_VMEM_BYTES = 64 * 1024 * 1024
_VMEM_LIMIT = _V7X_VMEM_BYTES - 8 * 1024 * 1024
_LANES = 128
_MOD_ROWS = 16


class _Cfg(NamedTuple):
    d: int
    n_p: int
    s_p: int
    b_s: int
    s_s: int
    past: int
    depth: int
    q_lora: int
    kv_lora: int
    rope_a: int
    nope_a: int
    v_a: int
    h_a: int
    h_b: int
    kv_b: int
    hd_b: int
    c_conv: int
    conv_w: int
    n_exp: int
    d_exp: int

    @property
    def n_s(self):
        return self.b_s * self.s_s

    @property
    def n(self):
        return self.n_p + self.n_s

    @property
    def tk_s(self):
        return self.past + self.s_s


def _pick(n, candidates):
    for c in candidates:
        if n % c == 0:
            return c
    raise ValueError(f"no tile in {candidates} divides {n}")


def _row_tile(cfg, candidates):
    for c in candidates:
        if cfg.n_p % c == 0 and cfg.s_s % c == 0:
            return c
    raise ValueError(f"no row tile in {candidates}")


def _cparams(*sem):
    return pltpu.CompilerParams(dimension_semantics=sem, vmem_limit_bytes=_VMEM_LIMIT)


def _group_of_tile(i, bm, cfg):
    r0 = i * bm
    return jnp.where(r0 < cfg.n_p, 0, 1 + (r0 - cfg.n_p) // cfg.s_s)


def _ada_kernel(s_ref, w_ref, b_ref, o_ref):
    w = w_ref[0].astype(_BF16)
    o_ref[0] = jnp.dot(s_ref[...], w, preferred_element_type=_F32) + b_ref[0]


def _ada(sm, w_ada, b_ada):
    depth, d, n6 = w_ada.shape
    bn = _pick(n6, (512, 256, 128))
    return pl.pallas_call(
        _ada_kernel,
        out_shape=jax.ShapeDtypeStruct((depth, _MOD_ROWS, n6), _F32),
        grid=(depth, n6 // bn),
        in_specs=[
            pl.BlockSpec((_MOD_ROWS, d), lambda l, j: (0, 0)),
            pl.BlockSpec((1, d, bn), lambda l, j: (l, 0, j)),
            pl.BlockSpec((1, 1, bn), lambda l, j: (l, 0, j)),
        ],
        out_specs=pl.BlockSpec((1, _MOD_ROWS, bn), lambda l, j: (l, 0, j)),
        compiler_params=_cparams("arbitrary", "arbitrary"),
        name="ada",
    )(sm, w_ada, b_ada.reshape(depth, 1, n6))


def _rms(x, g):
    return x * lax.rsqrt(jnp.mean(x * x, axis=-1, keepdims=True) + _EPS) * g


def _norm_mod_kernel(x_ref, g_ref, sc_ref, sh_ref, o_ref):
    h = _rms(x_ref[...], g_ref[...]) * (1.0 + sc_ref[0]) + sh_ref[0]
    o_ref[...] = h.astype(o_ref.dtype)


def _norm_mod(x, g, sc, sh, cfg):
    n, d = x.shape
    bm = _row_tile(cfg, (256, 128))
    grp = lambda i: (_group_of_tile(i, bm, cfg), 0, 0)
    return pl.pallas_call(
        _norm_mod_kernel,
        out_shape=jax.ShapeDtypeStruct((n, d), _BF16),
        grid=(n // bm,),
        in_specs=[
            pl.BlockSpec((bm, d), lambda i: (i, 0)),
            pl.BlockSpec((1, d), lambda i: (0, 0)),
            pl.BlockSpec((1, 1, d), grp),
            pl.BlockSpec((1, 1, d), grp),
        ],
        out_specs=pl.BlockSpec((bm, d), lambda i: (i, 0)),
        compiler_params=_cparams("arbitrary"),
        name="norm_mod",
    )(x, g.reshape(1, d), sc, sh)


def _final_norm_kernel(x_ref, g_ref, o_ref):
    o_ref[...] = _rms(x_ref[...], g_ref[...])


def _final_norm(x, g):
    n, d = x.shape
    bm = _pick(n, (256, 128))
    return pl.pallas_call(
        _final_norm_kernel,
        out_shape=jax.ShapeDtypeStruct((n, d), _F32),
        grid=(n // bm,),
        in_specs=[pl.BlockSpec((bm, d), lambda i: (i, 0)), pl.BlockSpec((1, d), lambda i: (0, 0))],
        out_specs=pl.BlockSpec((bm, d), lambda i: (i, 0)),
        compiler_params=_cparams("arbitrary"),
        name="final_norm",
    )(x, g.reshape(1, d))


def _mm_kernel(x_ref, w_ref, o_ref):
    o_ref[...] = jnp.dot(x_ref[...], w_ref[...], preferred_element_type=_F32).astype(o_ref.dtype)


def _matmul(x, w, out_dtype, name):
    m, k = x.shape
    n = w.shape[1]
    bm = _pick(m, (1024, 512, 256, 128))
    bn = _pick(n, (512, 256, 128))
    return pl.pallas_call(
        _mm_kernel,
        out_shape=jax.ShapeDtypeStruct((m, n), out_dtype),
        grid=(n // bn, m // bm),
        in_specs=[pl.BlockSpec((bm, k), lambda j, i: (i, 0)), pl.BlockSpec((k, bn), lambda j, i: (0, j))],
        out_specs=pl.BlockSpec((bm, bn), lambda j, i: (i, j)),
        compiler_params=_cparams("arbitrary", "arbitrary"),
        name=name,
    )(x, w)


class _InCols(NamedTuple):
    q_lat: int
    kv_lat: int
    q_b: int
    k_b: int
    v_b: int
    u_c: int
    k_pe: int
    k_pe_rot: int
    width: int


def _in_cols(cfg):
    offs, acc = [], 0
    for w in (cfg.q_lora, cfg.kv_lora, cfg.h_b * cfg.hd_b, cfg.kv_b * cfg.hd_b, cfg.kv_b * cfg.hd_b,
              2 * cfg.c_conv, cfg.rope_a, cfg.rope_a):
        offs.append(acc)
        acc += w
    width = -(-acc // 512) * 512
    return _InCols(*offs, width)


def _rotate_half_cols(w):
    half = w.shape[-1] // 2
    return jnp.concatenate([-w[..., half:], w[..., :half]], axis=-1)


def _reorder_w_in(w_in, cfg):
    ql, kvl, r = cfg.q_lora, cfg.kv_lora, cfg.rope_a
    o = ql + kvl
    k_pe = w_in[..., o:o + r]
    parts = [w_in[..., :o], w_in[..., o + r:], k_pe, _rotate_half_cols(k_pe)]
    cols = _in_cols(cfg)
    pad = cols.width - (w_in.shape[-1] + r)
    if pad:
        parts.append(jnp.zeros(w_in.shape[:-1] + (pad,), w_in.dtype))
    return jnp.concatenate(parts, axis=-1).astype(_BF16)


def _reorder_w_uq(w_uq, cfg):
    depth, ql, _ = w_uq.shape
    w = w_uq.reshape(depth, ql, cfg.h_a, cfg.nope_a + cfg.rope_a)
    nope = w[..., :cfg.nope_a].reshape(depth, ql, -1)
    pe = w[..., cfg.nope_a:]
    rot = _rotate_half_cols(pe).reshape(depth, ql, -1)
    return jnp.concatenate([nope, pe.reshape(depth, ql, -1), rot], axis=-1).astype(_BF16)


def _post_kernel(p_ref, wuq_ref, gq_ref, gkv_ref, gqb_ref, gkb_ref, ca_ref, sa_ref, cb_ref, sb_ref,
                 qa_ref, ckv_ref, kpe_ref, qb_ref, kb_ref, vb_ref, glu_ref, *, cfg, cols):
    ra, hd = cfg.rope_a, cfg.hd_b
    ca, sa, cb, sb = ca_ref[...], sa_ref[...], cb_ref[...], sb_ref[...]

    qn = _rms(p_ref[:, cols.q_lat:cols.q_lat + cfg.q_lora], gq_ref[...]).astype(_BF16)
    qa = jnp.dot(qn, wuq_ref[...], preferred_element_type=_F32)
    pe0 = cfg.h_a * cfg.nope_a
    rot0 = pe0 + cfg.h_a * ra
    scale_a = (cfg.nope_a + ra) ** -0.5
    for h in range(cfg.h_a):
        nope = qa[:, h * cfg.nope_a:(h + 1) * cfg.nope_a]
        pe = qa[:, pe0 + h * ra:pe0 + (h + 1) * ra] * ca + qa[:, rot0 + h * ra:rot0 + (h + 1) * ra] * sa
        qa_ref[h] = (jnp.concatenate([nope, pe], axis=-1) * scale_a).astype(qa_ref.dtype)

    ckv_ref[...] = _rms(p_ref[:, cols.kv_lat:cols.kv_lat + cfg.kv_lora], gkv_ref[...])
    kpe_ref[...] = p_ref[:, cols.k_pe:cols.k_pe + ra] * ca + p_ref[:, cols.k_pe_rot:cols.k_pe_rot + ra] * sa

    def head_norm_rope(x, g):
        y = _rms(x, g)
        return y * cb + pltpu.roll(y, hd // 2, axis=1) * sb

    scale_b = hd ** -0.5
    for h in range(cfg.h_b):
        x = p_ref[:, cols.q_b + h * hd:cols.q_b + (h + 1) * hd]
        qb_ref[h] = (head_norm_rope(x, gqb_ref[...]) * scale_b).astype(qb_ref.dtype)
    for h in range(cfg.kv_b):
        x = p_ref[:, cols.k_b + h * hd:cols.k_b + (h + 1) * hd]
        kb_ref[:, h * hd:(h + 1) * hd] = head_norm_rope(x, gkb_ref[...])
    vb_ref[...] = p_ref[:, cols.v_b:cols.v_b + cfg.kv_b * hd]

    a = p_ref[:, cols.u_c:cols.u_c + cfg.c_conv]
    gate = p_ref[:, cols.u_c + cfg.c_conv:cols.u_c + 2 * cfg.c_conv]
    glu_ref[...] = a * jax.nn.sigmoid(gate)


def _post_projection(p, w_uq_r, g_q_lat, g_kv_lat, g_q_b, g_k_b, tabs, cfg):
    n = cfg.n
    cols = _in_cols(cfg)
    bm = _pick(n, (256, 128))
    dk_a = cfg.nope_a + cfg.rope_a
    kvw = cfg.kv_b * cfg.hd_b
    row = lambda w: pl.BlockSpec((bm, w), lambda i: (i, 0))
    full = lambda a: pl.BlockSpec(a.shape, lambda i: (0,) * a.ndim)
    vecs = [g_q_lat.reshape(1, -1), g_kv_lat.reshape(1, -1), g_q_b.reshape(1, -1), g_k_b.reshape(1, -1)]
    out_shape = (
        jax.ShapeDtypeStruct((cfg.h_a, n, dk_a), _BF16),
        jax.ShapeDtypeStruct((n, cfg.kv_lora), _F32),
        jax.ShapeDtypeStruct((n, cfg.rope_a), _F32),
        jax.ShapeDtypeStruct((cfg.h_b, n, cfg.hd_b), _BF16),
        jax.ShapeDtypeStruct((n, kvw), _F32),
        jax.ShapeDtypeStruct((n, kvw), _F32),
        jax.ShapeDtypeStruct((n, cfg.c_conv), _F32),
    )
    out_specs = (
        pl.BlockSpec((cfg.h_a, bm, dk_a), lambda i: (0, i, 0)),
        row(cfg.kv_lora),
        row(cfg.rope_a),
        pl.BlockSpec((cfg.h_b, bm, cfg.hd_b), lambda i: (0, i, 0)),
        row(kvw),
        row(kvw),
        row(cfg.c_conv),
    )
    return pl.pallas_call(
        functools.partial(_post_kernel, cfg=cfg, cols=cols),
        out_shape=out_shape,
        grid=(n // bm,),
        in_specs=[row(cols.width), full(w_uq_r)] + [full(v) for v in vecs]
        + [row(cfg.rope_a), row(cfg.rope_a), row(cfg.hd_b), row(cfg.hd_b)],
        out_specs=out_specs,
        compiler_params=_cparams("arbitrary"),
        name="post_projection",
    )(p, w_uq_r, *vecs, *tabs)


def _kv_a_kernel(ckv_ref, kpe_ref, w_ref, k_ref, v_ref, *, cfg):
    kv = jnp.dot(ckv_ref[...].astype(_BF16), w_ref[...], preferred_element_type=_F32)
    kpe = kpe_ref[...]
    hw = cfg.nope_a + cfg.v_a
    for h in range(cfg.h_a):
        k_ref[h] = jnp.concatenate([kv[:, h * hw:h * hw + cfg.nope_a], kpe], axis=-1).astype(k_ref.dtype)
        v_ref[h] = kv[:, h * hw + cfg.nope_a:(h + 1) * hw].astype(v_ref.dtype)


def _kv_a(ckv_keys, kpe_keys, w_ukv, cfg):
    t = ckv_keys.shape[0]
    bm = _pick(t, (256, 128))
    dk = cfg.nope_a + cfg.rope_a
    return pl.pallas_call(
        functools.partial(_kv_a_kernel, cfg=cfg),
        out_shape=(jax.ShapeDtypeStruct((cfg.h_a, t, dk), _BF16),
                   jax.ShapeDtypeStruct((cfg.h_a, t, cfg.v_a), _BF16)),
        grid=(t // bm,),
        in_specs=[
            pl.BlockSpec((bm, cfg.kv_lora), lambda i: (i, 0)),
            pl.BlockSpec((bm, cfg.rope_a), lambda i: (i, 0)),
            pl.BlockSpec(w_ukv.shape, lambda i: (0, 0)),
        ],
        out_specs=(pl.BlockSpec((cfg.h_a, bm, dk), lambda i: (0, i, 0)),
                   pl.BlockSpec((cfg.h_a, bm, cfg.v_a), lambda i: (0, i, 0))),
        compiler_params=_cparams("arbitrary"),
        name="kv_a",
    )(ckv_keys, kpe_keys, w_ukv)


def _attn_kernel(q_ref, k_ref, v_ref, o_ref, *, hb, group, dv, kv_head_major):
    for j in range(hb):
        kvj = j // group
        q = q_ref[j]
        if kv_head_major:
            k, v = k_ref[kvj], v_ref[kvj]
        else:
            k, v = k_ref[:, kvj * dv:(kvj + 1) * dv], v_ref[:, kvj * dv:(kvj + 1) * dv]
        s = lax.dot_general(q, k, (((1,), (1,)), ((), ())), preferred_element_type=_F32)
        m = jnp.max(s, axis=-1, keepdims=True)
        p = jnp.exp(s - m)
        l = jnp.sum(p, axis=-1, keepdims=True)
        o = jnp.dot(p.astype(_BF16), v, preferred_element_type=_F32) / l
        o_ref[:, j * dv:(j + 1) * dv] = o.astype(o_ref.dtype)


def _attention(q, k, v, *, n_seq, tq, tk, q_row0, k_row0, hb, group, dv, bq, kv_head_major, name):
    hq, _, dk = q.shape
    hkb = hb // group
    nq = tq // bq
    qb0, kb0 = q_row0 // bq, k_row0 // tk
    assert q_row0 % bq == 0 and k_row0 % tk == 0 and hq % hb == 0 and hb % group == 0
    if kv_head_major:
        k_spec = pl.BlockSpec((hkb, tk, k.shape[2]), lambda s, h, i: (h, kb0 + s, 0))
        v_spec = pl.BlockSpec((hkb, tk, v.shape[2]), lambda s, h, i: (h, kb0 + s, 0))
    else:
        k_spec = pl.BlockSpec((tk, hkb * dv), lambda s, h, i: (kb0 + s, h))
        v_spec = pl.BlockSpec((tk, hkb * dv), lambda s, h, i: (kb0 + s, h))
    return pl.pallas_call(
        functools.partial(_attn_kernel, hb=hb, group=group, dv=dv, kv_head_major=kv_head_major),
        out_shape=jax.ShapeDtypeStruct((n_seq * tq, hq * dv), _BF16),
        grid=(n_seq, hq // hb, nq),
        in_specs=[pl.BlockSpec((hb, bq, dk), lambda s, h, i: (h, qb0 + s * nq + i, 0)), k_spec, v_spec],
        out_specs=pl.BlockSpec((bq, hb * dv), lambda s, h, i: (s * nq + i, h)),
        compiler_params=_cparams("arbitrary", "arbitrary", "arbitrary"),
        name=name,
    )(q, k, v)


_HALO = 16
_CONV_ROWS = 32


def _conv_kernel(prev_ref, cur_ref, next_ref, w_ref, bdw_ref, g_ref, b_ref, o_ref, buf_ref, *, cfg, bt):
    i = pl.program_id(0)
    n_pt = cfg.n_p // bt
    per_seq = cfg.s_s // bt
    j = jnp.maximum(i - n_pt, 0) % per_seq
    lat = i >= n_pt
    p_per = cfg.s_p // bt
    jp = i % p_per
    has_prev = jnp.where(lat, j > 0, jp > 0)
    has_next = jnp.where(lat, j < per_seq - 1, jp < p_per - 1)
    buf_ref[0:_HALO, :] = jnp.where(has_prev, prev_ref[...], 0.0)
    buf_ref[_HALO:_HALO + bt, :] = cur_ref[...]
    buf_ref[_HALO + bt:2 * _HALO + bt, :] = jnp.where(has_next, next_ref[...], 0.0)
    half = cfg.conv_w // 2
    for r in range(0, bt, _CONV_ROWS):
        acc = jnp.zeros((_CONV_ROWS, cfg.c_conv), _F32)
        for t in range(cfg.conv_w):
            lo = _HALO + r + t - half
            acc = acc + buf_ref[lo:lo + _CONV_ROWS, :] * w_ref[t:t + 1, :]
        y = acc + bdw_ref[...]
        mu = jnp.mean(y, axis=-1, keepdims=True)
        yc = y - mu
        var = jnp.mean(yc * yc, axis=-1, keepdims=True)
        z = yc * lax.rsqrt(var + _EPS) * g_ref[...] + b_ref[...]
        o_ref[r:r + _CONV_ROWS, :] = (z * jax.nn.sigmoid(z)).astype(o_ref.dtype)


def _conv_module(v, w_dw, b_dw, g_ln, b_ln, cfg):
    n, c = v.shape
    bt = _pick(cfg.s_p, (256, 128))
    assert cfg.s_s % bt == 0 and cfg.conv_w // 2 < _HALO and bt % _CONV_ROWS == 0
    hb = bt // _HALO
    last = n // _HALO - 1
    vec = lambda a: pl.BlockSpec((1, c), lambda i: (0, 0))
    return pl.pallas_call(
        functools.partial(_conv_kernel, cfg=cfg, bt=bt),
        out_shape=jax.ShapeDtypeStruct((n, c), _BF16),
        grid=(n // bt,),
        in_specs=[
            pl.BlockSpec((_HALO, c), lambda i: (jnp.maximum(i * hb - 1, 0), 0)),
            pl.BlockSpec((bt, c), lambda i: (i, 0)),
            pl.BlockSpec((_HALO, c), lambda i: (jnp.minimum((i + 1) * hb, last), 0)),
            pl.BlockSpec(w_dw.shape, lambda i: (0, 0)),
            vec(b_dw), vec(g_ln), vec(b_ln),
        ],
        out_specs=pl.BlockSpec((bt, c), lambda i: (i, 0)),
        scratch_shapes=[pltpu.VMEM((bt + 2 * _HALO, c), _F32)],
        compiler_params=_cparams("arbitrary"),
        name="conv_module",
    )(v, v, v, w_dw, b_dw.reshape(1, c), g_ln.reshape(1, c), b_ln.reshape(1, c))


def _out_proj_kernel(a_ref, b_ref, c_ref, w_ref, x_ref, g_ref, o_ref, wb_ref, *, wa, wbw):
    @pl.when(pl.program_id(1) == 0)
    def _():
        wb_ref[...] = w_ref[...].astype(_BF16)

    acc = jnp.dot(a_ref[...], wb_ref[0:wa, :], preferred_element_type=_F32)
    acc = acc + jnp.dot(b_ref[...], wb_ref[wa:wa + wbw, :], preferred_element_type=_F32)
    acc = acc + jnp.dot(c_ref[...], wb_ref[wa + wbw:, :], preferred_element_type=_F32)
    o_ref[...] = x_ref[...] + g_ref[0] * acc


def _out_proj(out_a, out_b, out_c, w_out, x, gate, cfg):
    n, d = x.shape
    wa, wbw, wc = out_a.shape[1], out_b.shape[1], out_c.shape[1]
    bm = _row_tile(cfg, (1024, 512, 256, 128))
    bn = _pick(d, (512, 256, 128))
    return pl.pallas_call(
        functools.partial(_out_proj_kernel, wa=wa, wbw=wbw),
        out_shape=jax.ShapeDtypeStruct((n, d), _F32),
        grid=(d // bn, n // bm),
        in_specs=[
            pl.BlockSpec((bm, wa), lambda j, i: (i, 0)),
            pl.BlockSpec((bm, wbw), lambda j, i: (i, 0)),
            pl.BlockSpec((bm, wc), lambda j, i: (i, 0)),
            pl.BlockSpec((d, bn), lambda j, i: (0, j)),
            pl.BlockSpec((bm, bn), lambda j, i: (i, j)),
            pl.BlockSpec((1, 1, bn), lambda j, i: (_group_of_tile(i, bm, cfg), 0, j)),
        ],
        out_specs=pl.BlockSpec((bm, bn), lambda j, i: (i, j)),
        scratch_shapes=[pltpu.VMEM((d, bn), _BF16)],
        compiler_params=_cparams("arbitrary", "arbitrary"),
        name="out_proj",
    )(out_a, out_b, out_c, w_out, x, gate)


def _norm_router_kernel(x_ref, g_ref, sc_ref, sh_ref, wr_ref, br_ref, h_ref, idx_ref, wts_ref, *, n_exp):
    h = _rms(x_ref[...], g_ref[...]) * (1.0 + sc_ref[0]) + sh_ref[0]
    h_ref[...] = h
    logits = jnp.dot(h.astype(_BF16), wr_ref[...], preferred_element_type=_F32)
    scores = jax.nn.sigmoid(logits)
    biased = scores + br_ref[...]
    sc_t = scores.T
    bi_t = biased.T
    epg = n_exp // _N_GROUPS
    row = lambda a, e: a[e:e + 1, :]

    group_scores = []
    for g in range(_N_GROUPS):
        vals = [row(bi_t, g * epg + i) for i in range(epg)]
        best = None
        for i in range(epg):
            for j in range(i + 1, epg):
                pair = vals[i] + vals[j]
                best = pair if best is None else jnp.maximum(best, pair)
        group_scores.append(best)
    sel = jnp.zeros_like(group_scores[0], dtype=jnp.int32)
    best = group_scores[0]
    for g in range(1, _N_GROUPS):
        upd = group_scores[g] > best
        sel = jnp.where(upd, g, sel)
        best = jnp.where(upd, group_scores[g], best)

    def in_group(a, i):
        out = row(a, i)
        for g in range(1, _N_GROUPS):
            out = jnp.where(sel == g, row(a, g * epg + i), out)
        return out

    b = [in_group(bi_t, i) for i in range(epg)]
    u = [in_group(sc_t, i) for i in range(epg)]
    i1, v1, w1 = jnp.zeros_like(sel), b[0], u[0]
    for i in range(1, epg):
        upd = b[i] > v1
        i1, v1, w1 = jnp.where(upd, i, i1), jnp.where(upd, b[i], v1), jnp.where(upd, u[i], w1)
    i2 = jnp.zeros_like(sel)
    v2 = jnp.full_like(v1, -jnp.inf)
    w2 = jnp.zeros_like(w1)
    for i in range(epg):
        upd = jnp.logical_and(i1 != i, b[i] > v2)
        i2, v2, w2 = jnp.where(upd, i, i2), jnp.where(upd, b[i], v2), jnp.where(upd, u[i], w2)
    den = w1 + w2
    idx_ref[0:1, :] = sel * epg + i1
    idx_ref[1:2, :] = sel * epg + i2
    wts_ref[0:1, :] = w1 / den
    wts_ref[1:2, :] = w2 / den


def _norm_router(x, g, sc, sh, w_router, router_bias, cfg):
    n, d = x.shape
    bm = _row_tile(cfg, (256, 128))
    e = cfg.n_exp
    wr = jnp.zeros((d, _LANES), _BF16).at[:, :e].set(w_router.astype(_BF16))
    br = jnp.zeros((1, _LANES), _F32).at[0, :e].set(router_bias)
    grp = lambda i: (_group_of_tile(i, bm, cfg), 0, 0)
    return pl.pallas_call(
        functools.partial(_norm_router_kernel, n_exp=e),
        out_shape=(jax.ShapeDtypeStruct((n, d), _F32),
                   jax.ShapeDtypeStruct((2, n), jnp.int32),
                   jax.ShapeDtypeStruct((2, n), _F32)),
        grid=(n // bm,),
        in_specs=[
            pl.BlockSpec((bm, d), lambda i: (i, 0)),
            pl.BlockSpec((1, d), lambda i: (0, 0)),
            pl.BlockSpec((1, 1, d), grp),
            pl.BlockSpec((1, 1, d), grp),
            pl.BlockSpec((d, _LANES), lambda i: (0, 0)),
            pl.BlockSpec((1, _LANES), lambda i: (0, 0)),
        ],
        out_specs=(pl.BlockSpec((bm, d), lambda i: (i, 0)),
                   pl.BlockSpec((2, bm), lambda i: (0, i)),
                   pl.BlockSpec((2, bm), lambda i: (0, i))),
        compiler_params=_cparams("arbitrary"),
        name="norm_router",
    )(x, g.reshape(1, d), sc, sh, wr, br)


class _Plan(NamedTuple):
    row_tok: jax.Array
    row_w: jax.Array
    pos: jax.Array
    tile_e: jax.Array
    tile_new: jax.Array
    tile_ok: jax.Array


def _route_plan(idx, wts, n_exp, tm):
    n = idx.shape[1]
    e_flat = idx.reshape(-1)
    onehot = (e_flat[:, None] == jnp.arange(n_exp, dtype=jnp.int32)[None, :]).astype(jnp.int32)
    rank = jnp.sum((jnp.cumsum(onehot, axis=0) - onehot) * onehot, axis=1)
    counts = jnp.sum(onehot, axis=0)
    tiles_per = (counts + tm - 1) // tm
    tile_end = jnp.cumsum(tiles_per)
    tile_start = tile_end - tiles_per
    pos = tile_start[e_flat] * tm + rank
    n_tiles = (2 * n) // tm + n_exp
    rows = n_tiles * tm
    tok = jnp.arange(2 * n, dtype=jnp.int32) % n
    row_tok = jnp.zeros((rows,), jnp.int32).at[pos].set(tok)
    row_w = jnp.zeros((rows,), _F32).at[pos].set(wts.reshape(-1)).reshape(rows, 1)
    tid = jnp.arange(n_tiles, dtype=jnp.int32)
    total = tile_end[-1]
    ok = tid < total
    te = jnp.searchsorted(tile_end, jnp.minimum(tid, total - 1), side="right").astype(jnp.int32)
    new = jnp.concatenate([jnp.ones((1,), jnp.int32), (te[1:] != te[:-1]).astype(jnp.int32)])
    return _Plan(row_tok, row_w, pos.astype(jnp.int32), te, new, ok.astype(jnp.int32))


_GATHER_UNROLL = 8


def _gather_kernel(tok_ref, h_hbm, o_ref, buf_ref, sem, *, tg):
    base = pl.program_id(0) * tg

    def row_copy(r, tok):
        return pltpu.make_async_copy(h_hbm.at[pl.ds(tok, 1)], buf_ref.at[pl.ds(r, 1)], sem)

    def start(r, c):
        row_copy(r, tok_ref[base + r]).start()
        return c

    def wait(r, c):
        row_copy(r, 0).wait()
        return c

    lax.fori_loop(0, tg, start, 0, unroll=_GATHER_UNROLL)
    lax.fori_loop(0, tg, wait, 0, unroll=_GATHER_UNROLL)
    o_ref[...] = buf_ref[...].astype(o_ref.dtype)


def _gather_rows(h, row_tok, tg):
    rows = row_tok.shape[0]
    d = h.shape[1]
    return pl.pallas_call(
        functools.partial(_gather_kernel, tg=tg),
        out_shape=jax.ShapeDtypeStruct((rows, d), _BF16),
        grid_spec=pltpu.PrefetchScalarGridSpec(
            num_scalar_prefetch=1,
            grid=(rows // tg,),
            in_specs=[pl.BlockSpec(memory_space=pl.ANY)],
            out_specs=pl.BlockSpec((tg, d), lambda i, tok: (i, 0)),
            scratch_shapes=[pltpu.VMEM((tg, d), h.dtype), pltpu.SemaphoreType.DMA(())],
        ),
        compiler_params=_cparams("arbitrary"),
        name="moe_gather",
    )(row_tok, h)


def _gate_up_kernel(te_ref, new_ref, ok_ref, x_ref, wg_ref, wu_ref, o_ref, wgb_ref, wub_ref):
    t = pl.program_id(1)

    @pl.when(new_ref[t] == 1)
    def _():
        wgb_ref[...] = wg_ref[0].astype(_BF16)
        wub_ref[...] = wu_ref[0].astype(_BF16)

    @pl.when(ok_ref[t] == 1)
    def _():
        x = x_ref[...]
        a = jnp.dot(x, wgb_ref[...], preferred_element_type=_F32)
        b = jnp.dot(x, wub_ref[...], preferred_element_type=_F32)
        o_ref[...] = (a * jax.nn.sigmoid(a) * b).astype(o_ref.dtype)

    @pl.when(ok_ref[t] == 0)
    def _():
        o_ref[...] = jnp.zeros_like(o_ref)


def _gate_up(xs, w_gate, w_up, plan, tm):
    rows, d = xs.shape
    de = w_gate.shape[2]
    bn = _pick(de, (256, 128))
    n_tiles = rows // tm
    w_spec = pl.BlockSpec((1, d, bn), lambda j, t, te, new, ok: (te[t], 0, j))
    return pl.pallas_call(
        _gate_up_kernel,
        out_shape=jax.ShapeDtypeStruct((rows, de), _BF16),
        grid_spec=pltpu.PrefetchScalarGridSpec(
            num_scalar_prefetch=3,
            grid=(de // bn, n_tiles),
            in_specs=[pl.BlockSpec((tm, d), lambda j, t, te, new, ok: (t, 0)), w_spec, w_spec],
            out_specs=pl.BlockSpec((tm, bn), lambda j, t, te, new, ok: (t, j)),
            scratch_shapes=[pltpu.VMEM((d, bn), _BF16), pltpu.VMEM((d, bn), _BF16)],
        ),
        compiler_params=_cparams("arbitrary", "arbitrary"),
        name="moe_gate_up",
    )(plan.tile_e, plan.tile_new, plan.tile_ok, xs, w_gate, w_up)


def _down_kernel(te_ref, new_ref, ok_ref, x_ref, w_ref, rw_ref, o_ref, wb_ref):
    t = pl.program_id(1)

    @pl.when(new_ref[t] == 1)
    def _():
        wb_ref[...] = w_ref[0].astype(_BF16)

    @pl.when(ok_ref[t] == 1)
    def _():
        o_ref[...] = jnp.dot(x_ref[...], wb_ref[...], preferred_element_type=_F32) * rw_ref[...]

    @pl.when(ok_ref[t] == 0)
    def _():
        o_ref[...] = jnp.zeros_like(o_ref)


def _down(hm, w_down, plan, tm):
    rows, de = hm.shape
    d = w_down.shape[2]
    bn = _pick(d, (1024, 512, 256, 128))
    n_tiles = rows // tm
    return pl.pallas_call(
        _down_kernel,
        out_shape=jax.ShapeDtypeStruct((rows, d), _F32),
        grid_spec=pltpu.PrefetchScalarGridSpec(
            num_scalar_prefetch=3,
            grid=(d // bn, n_tiles),
            in_specs=[
                pl.BlockSpec((tm, de), lambda j, t, te, new, ok: (t, 0)),
                pl.BlockSpec((1, de, bn), lambda j, t, te, new, ok: (te[t], 0, j)),
                pl.BlockSpec((tm, 1), lambda j, t, te, new, ok: (t, 0)),
            ],
            out_specs=pl.BlockSpec((tm, bn), lambda j, t, te, new, ok: (t, j)),
            scratch_shapes=[pltpu.VMEM((de, bn), _BF16)],
        ),
        compiler_params=_cparams("arbitrary", "arbitrary"),
        name="moe_down",
    )(plan.tile_e, plan.tile_new, plan.tile_ok, hm, w_down, plan.row_w)


def _combine_kernel(pos_ref, y_hbm, x_ref, g_ref, o_ref, buf_ref, sem, *, bt, n):
    base = pl.program_id(0) * bt

    def row_copy(k, r, p):
        return pltpu.make_async_copy(y_hbm.at[pl.ds(p, 1)], buf_ref.at[k, pl.ds(r, 1)], sem)

    def start(r, c):
        row_copy(0, r, pos_ref[base + r]).start()
        row_copy(1, r, pos_ref[n + base + r]).start()
        return c

    def wait(r, c):
        row_copy(0, r, 0).wait()
        row_copy(1, r, 0).wait()
        return c

    lax.fori_loop(0, bt, start, 0, unroll=_GATHER_UNROLL)
    lax.fori_loop(0, bt, wait, 0, unroll=_GATHER_UNROLL)
    o_ref[...] = x_ref[...] + g_ref[0] * (buf_ref[0] + buf_ref[1])


def _combine(y, pos, x, gate, cfg):
    n, d = x.shape
    bt = _row_tile(cfg, (256, 128))
    return pl.pallas_call(
        functools.partial(_combine_kernel, bt=bt, n=n),
        out_shape=jax.ShapeDtypeStruct((n, d), _F32),
        grid_spec=pltpu.PrefetchScalarGridSpec(
            num_scalar_prefetch=1,
            grid=(n // bt,),
            in_specs=[
                pl.BlockSpec(memory_space=pl.ANY),
                pl.BlockSpec((bt, d), lambda i, pos: (i, 0)),
                pl.BlockSpec((1, 1, d), lambda i, pos: (_group_of_tile(i, bt, cfg), 0, 0)),
            ],
            out_specs=pl.BlockSpec((bt, d), lambda i, pos: (i, 0)),
            scratch_shapes=[pltpu.VMEM((2, bt, d), _F32), pltpu.SemaphoreType.DMA(())],
        ),
        compiler_params=_cparams("arbitrary"),
        name="moe_combine",
    )(pos, y, x, gate)


def _rope_tables(cfg):
    t = cfg.s_s
    pos = jnp.arange(t, dtype=jnp.int32)
    row = (pos // _GRID_W).astype(_F32)
    col = (pos % _GRID_W).astype(_F32)

    def angles(rot_dim):
        quarter = rot_dim // 4
        inv_freq = _ROPE_THETA ** (-jnp.arange(quarter, dtype=_F32) / quarter)
        return jnp.concatenate([row[:, None] * inv_freq, col[:, None] * inv_freq], axis=-1)

    def stack(prompt_val, lat):
        lat = jnp.tile(lat, (cfg.b_s, 1))
        return jnp.concatenate([jnp.full((cfg.n_p, lat.shape[1]), prompt_val, _F32), lat], axis=0)

    ang_a, ang_b = angles(cfg.rope_a), angles(cfg.hd_b)
    cos_a, sin_a = jnp.cos(ang_a), jnp.sin(ang_a)
    cos_b, sin_b = jnp.cos(ang_b), jnp.sin(ang_b)
    return (stack(1.0, jnp.concatenate([cos_a, cos_a], -1)), stack(0.0, jnp.concatenate([sin_a, sin_a], -1)),
            stack(1.0, jnp.concatenate([cos_b, cos_b], -1)), stack(0.0, jnp.concatenate([-sin_b, sin_b], -1)))


def _keys_with_cache(own, cache, cfg):
    parts = []
    for b in range(cfg.b_s):
        parts += [cache[b], own[cfg.n_p + b * cfg.s_s:cfg.n_p + (b + 1) * cfg.s_s]]
    parts.append(own[:cfg.n_p])
    return jnp.concatenate(parts, axis=0)


def _make_cfg(x_prompt, x_sample, cache_mla_ckv, cache_mla_krope, cache_gqa_k, w_ada, w_in, g_q_lat,
              w_uq, w_ukv, w_dw, w_router, w_gate):
    b, s, d = x_prompt.shape
    b_s, s_s, _ = x_sample.shape
    past, kv_lora = cache_mla_ckv.shape[2:]
    rope_a = cache_mla_krope.shape[3]
    kv_b, hd_b = cache_gqa_k.shape[3:]
    q_lora = g_q_lat.shape[1]
    conv_w, c_conv = w_dw.shape[1:]
    h_b = (w_in.shape[2] - q_lora - kv_lora - rope_a - 2 * kv_b * hd_b - 2 * c_conv) // hd_b
    hv = d - h_b * hd_b - c_conv
    h_a = (w_uq.shape[2] - w_ukv.shape[2] + hv) // rope_a
    v_a = hv // h_a
    nope_a = w_ukv.shape[2] // h_a - v_a
    return _Cfg(d=d, n_p=b * s, s_p=s, b_s=b_s, s_s=s_s, past=past, depth=w_ada.shape[0], q_lora=q_lora,
                kv_lora=kv_lora, rope_a=rope_a, nope_a=nope_a, v_a=v_a, h_a=h_a, h_b=h_b, kv_b=kv_b,
                hd_b=hd_b, c_conv=c_conv, conv_w=conv_w, n_exp=w_router.shape[1], d_exp=w_gate.shape[3])


def kernel(x_prompt, x_sample, c, c_ctx, cache_mla_ckv, cache_mla_krope, cache_gqa_k, cache_gqa_v, w_ada, b_ada, g_norm1, g_norm2, w_in, g_q_lat, g_kv_lat, w_uq, w_ukv, g_q_b, g_k_b, w_dw, b_dw, g_conv_ln, b_conv_ln, w_out, w_router, router_bias, w_gate, w_up, w_down, g_final):
    cfg = _make_cfg(x_prompt, x_sample, cache_mla_ckv, cache_mla_krope, cache_gqa_k, w_ada, w_in, g_q_lat,
                    w_uq, w_ukv, w_dw, w_router, w_gate)
    d, n_p, depth = cfg.d, cfg.n_p, cfg.depth
    batch, seq = x_prompt.shape[:2]
    kvw = cfg.kv_b * cfg.hd_b
    assert cfg.v_a == cfg.hd_b and 1 + cfg.b_s <= _MOD_ROWS

    x = jnp.concatenate([x_prompt.reshape(n_p, d), x_sample.reshape(cfg.n_s, d)], axis=0)

    m = jnp.concatenate([c_ctx[None, :], c, jnp.zeros((_MOD_ROWS - 1 - cfg.b_s, d), _F32)], axis=0)
    mod = _ada(jax.nn.silu(m).astype(_BF16), w_ada, b_ada).reshape(depth, _MOD_ROWS, 6, 1, d)

    w_in_r = _reorder_w_in(w_in, cfg)
    w_uq_r = _reorder_w_uq(w_uq, cfg)
    w_ukv_b = w_ukv.astype(_BF16)
    tabs = _rope_tables(cfg)
    tm = _pick(2 * cfg.n, (512, 256, 128))
    bq_s = _pick(cfg.s_s, (256, 128))

    ckvs, kpes, kbs, vbs = [], [], [], []
    for l in range(depth):
        sh1, sc1, g1, sh2, sc2, g2 = [mod[l, :, i] for i in range(6)]
        h = _norm_mod(x, g_norm1[l], sc1, sh1, cfg)
        p = _matmul(h, w_in_r[l], _F32, "in_proj")
        q_a, ckv, kpe, q_b, k_b, v_b, glu = _post_projection(
            p, w_uq_r[l], g_q_lat[l], g_kv_lat[l], g_q_b[l], g_k_b[l], tabs, cfg)
        ckvs.append(ckv[:n_p])
        kpes.append(kpe[:n_p])
        kbs.append(k_b[:n_p])
        vbs.append(v_b[:n_p])

        k_a, v_a = _kv_a(_keys_with_cache(ckv, cache_mla_ckv[:, l], cfg),
                         _keys_with_cache(kpe, cache_mla_krope[:, l], cfg), w_ukv_b[l], cfg)
        common_a = dict(group=1, dv=cfg.v_a, kv_head_major=True)
        a_p = _attention(q_a, k_a, v_a, n_seq=batch, tq=seq, tk=seq, q_row0=0, k_row0=cfg.b_s * cfg.tk_s,
                         hb=cfg.h_a, bq=seq, name="attn_a_prompt", **common_a)
        a_s = _attention(q_a, k_a, v_a, n_seq=cfg.b_s, tq=cfg.s_s, tk=cfg.tk_s, q_row0=n_p, k_row0=0,
                         hb=1, bq=bq_s, name="attn_a_latent", **common_a)
        k_bk = _keys_with_cache(k_b, cache_gqa_k[:, l].reshape(cfg.b_s, cfg.past, kvw), cfg).astype(_BF16)
        v_bk = _keys_with_cache(v_b, cache_gqa_v[:, l].reshape(cfg.b_s, cfg.past, kvw), cfg).astype(_BF16)
        grp = cfg.h_b // cfg.kv_b
        common_b = dict(group=grp, dv=cfg.hd_b, kv_head_major=False)
        b_p = _attention(q_b, k_bk, v_bk, n_seq=batch, tq=seq, tk=seq, q_row0=0, k_row0=cfg.b_s * cfg.tk_s,
                         hb=cfg.h_b, bq=seq, name="attn_b_prompt", **common_b)
        b_s = _attention(q_b, k_bk, v_bk, n_seq=cfg.b_s, tq=cfg.s_s, tk=cfg.tk_s, q_row0=n_p, k_row0=0,
                         hb=grp, bq=bq_s, name="attn_b_latent", **common_b)
        out_c = _conv_module(glu, w_dw[l], b_dw[l], g_conv_ln[l], b_conv_ln[l], cfg)

        x = _out_proj(jnp.concatenate([a_p, a_s], axis=0), jnp.concatenate([b_p, b_s], axis=0), out_c,
                      w_out[l], x, g1, cfg)

        h2, idx, wts = _norm_router(x, g_norm2[l], sc2, sh2, w_router, router_bias, cfg)
        plan = _route_plan(idx, wts, cfg.n_exp, tm)
        xs = _gather_rows(h2, plan.row_tok, tm)
        hm = _gate_up(xs, w_gate[l], w_up[l], plan, tm)
        y = _down(hm, w_down[l], plan, tm)
        x = _combine(y, plan.pos, x, g2, cfg)

    y_all = _final_norm(x, g_final)
    stack = lambda parts, tail: jnp.stack([a.reshape((batch, seq) + tail) for a in parts], axis=1)
    return (y_all[:n_p].reshape(batch, seq, d),
            y_all[n_p:].reshape(cfg.b_s, cfg.s_s, d),
            stack(ckvs, (cfg.kv_lora,)),
            stack(kpes, (cfg.rope_a,)),
            stack(kbs, (cfg.kv_b, cfg.hd_b)),
            stack(vbs, (cfg.kv_b, cfg.hd_b)))
```

```python
import functools
from typing import NamedTuple

import jax
import jax.numpy as jnp
from jax import lax
from jax.experimental import pallas as pl
from jax.experimental.pallas import tpu as pltpu

_F32 = jnp.float32
_BF16 = jnp.bfloat16
_U32 = jnp.uint32

_GRID_W = 64
_N_GROUPS = 4
_ROPE_THETA = 10000.0
_EPS = 1e-6
_LOG2E = 1.4426950408889634

_V7X_VMEM_BYTES = 64 * 1024 * 1024
_VMEM_LIMIT = _V7X_VMEM_BYTES - 8 * 1024 * 1024
_LANES = 128
_SUBLANES = 8
_MOD_ROWS = 16
_SHIFT1, _SCALE1, _GATE1, _SHIFT2, _SCALE2, _GATE2 = range(6)


class _Cfg(NamedTuple):
    d: int
    n_p: int
    s_p: int
    b_s: int
    s_s: int
    past: int
    depth: int
    q_lora: int
    kv_lora: int
    rope_a: int
    nope_a: int
    v_a: int
    h_a: int
    h_b: int
    kv_b: int
    hd_b: int
    c_conv: int
    conv_w: int
    n_exp: int
    d_exp: int

    @property
    def n_s(self):
        return self.b_s * self.s_s

    @property
    def n(self):
        return self.n_p + self.n_s

    @property
    def tk_s(self):
        return self.past + self.s_s


def _pick(n, candidates):
    for c in candidates:
        if n % c == 0:
            return c
    raise ValueError(f"no tile in {candidates} divides {n}")


def _row_tile(cfg, candidates):
    for c in candidates:
        if cfg.n_p % c == 0 and cfg.s_s % c == 0:
            return c
    raise ValueError(f"no row tile in {candidates}")


def _cparams(*sem):
    return pltpu.CompilerParams(dimension_semantics=sem, vmem_limit_bytes=_VMEM_LIMIT)


def _group_of_tile(i, bm, cfg):
    r0 = i * bm
    return jnp.where(r0 < cfg.n_p, 0, 1 + (r0 - cfg.n_p) // cfg.s_s)


def _mod_spec(l, which, width, group_fn, col_fn=None):
    def index(*ids):
        return (l, group_fn(*ids), which, 0, 0 if col_fn is None else col_fn(*ids))
    return pl.BlockSpec((None, 1, None, 1, width), index)


def _layer_spec(a, l):
    nd = a.ndim - 1
    return pl.BlockSpec((None,) + a.shape[1:], lambda *ids: (l,) + (0,) * nd)


def _pack_bf16_pairs(x):
    w = x.shape[1] // 2
    r = x.astype(_BF16).astype(_F32)
    lo = lax.bitcast_convert_type(r[:, :w], _U32) >> 16
    hi = lax.bitcast_convert_type(r[:, w:], _U32) & _U32(0xFFFF0000)
    return lo | hi


def _unpack_bf16_pairs(p):
    lo = lax.bitcast_convert_type(p << 16, _F32)
    hi = lax.bitcast_convert_type(p & _U32(0xFFFF0000), _F32)
    return lo, hi


def _ada_kernel(s_ref, w_ref, b_ref, o_ref):
    w = w_ref[0].astype(_BF16)
    o_ref[0] = jnp.dot(s_ref[...], w, preferred_element_type=_F32) + b_ref[0]


def _ada(sm, w_ada, b_ada):
    depth, d, n6 = w_ada.shape
    bn = _pick(n6, (512, 256, 128))
    return pl.pallas_call(
        _ada_kernel,
        out_shape=jax.ShapeDtypeStruct((depth, _MOD_ROWS, n6), _F32),
        grid=(depth, n6 // bn),
        in_specs=[
            pl.BlockSpec((_MOD_ROWS, d), lambda l, j: (0, 0)),
            pl.BlockSpec((1, d, bn), lambda l, j: (l, 0, j)),
            pl.BlockSpec((1, 1, bn), lambda l, j: (l, 0, j)),
        ],
        out_specs=pl.BlockSpec((1, _MOD_ROWS, bn), lambda l, j: (l, 0, j)),
        compiler_params=_cparams("arbitrary", "arbitrary"),
        name="ada",
    )(sm, w_ada, b_ada.reshape(depth, 1, n6))


def _rms(x, g):
    return x * lax.rsqrt(jnp.mean(x * x, axis=-1, keepdims=True) + _EPS) * g


def _norm_mod_kernel(x_ref, g_ref, sc_ref, sh_ref, o_ref):
    h = _rms(x_ref[...], g_ref[...]) * (1.0 + sc_ref[0]) + sh_ref[0]
    o_ref[...] = h.astype(o_ref.dtype)


def _norm_mod(x, g, mod, l, cfg):
    n, d = x.shape
    bm = _row_tile(cfg, (256, 128))
    grp = lambda i: _group_of_tile(i, bm, cfg)
    return pl.pallas_call(
        _norm_mod_kernel,
        out_shape=jax.ShapeDtypeStruct((n, d), _BF16),
        grid=(n // bm,),
        in_specs=[
            pl.BlockSpec((bm, d), lambda i: (i, 0)),
            _layer_spec(g, l),
            _mod_spec(l, _SCALE1, d, grp),
            _mod_spec(l, _SHIFT1, d, grp),
        ],
        out_specs=pl.BlockSpec((bm, d), lambda i: (i, 0)),
        compiler_params=_cparams("arbitrary"),
        name="norm_mod",
    )(x, g, mod, mod)


def _final_norm_kernel(x_ref, g_ref, o_ref):
    o_ref[...] = _rms(x_ref[...], g_ref[...])


def _final_norm(x, g):
    n, d = x.shape
    bm = _pick(n, (256, 128))
    return pl.pallas_call(
        _final_norm_kernel,
        out_shape=jax.ShapeDtypeStruct((n, d), _F32),
        grid=(n // bm,),
        in_specs=[pl.BlockSpec((bm, d), lambda i: (i, 0)), pl.BlockSpec((1, d), lambda i: (0, 0))],
        out_specs=pl.BlockSpec((bm, d), lambda i: (i, 0)),
        compiler_params=_cparams("arbitrary"),
        name="final_norm",
    )(x, g.reshape(1, d))


def _mm_kernel(x_ref, w_ref, o_ref):
    o_ref[...] = jnp.dot(x_ref[...], w_ref[...], preferred_element_type=_F32).astype(o_ref.dtype)


def _matmul(x, w, l, out_dtype, name):
    m, k = x.shape
    n = w.shape[2]
    bm = _pick(m, (1024, 512, 256, 128))
    bn = _pick(n, (512, 256, 128))
    return pl.pallas_call(
        _mm_kernel,
        out_shape=jax.ShapeDtypeStruct((m, n), out_dtype),
        grid=(n // bn, m // bm),
        in_specs=[pl.BlockSpec((bm, k), lambda j, i: (i, 0)),
                  pl.BlockSpec((None, k, bn), lambda j, i: (l, 0, j))],
        out_specs=pl.BlockSpec((bm, bn), lambda j, i: (i, j)),
        compiler_params=_cparams("arbitrary", "arbitrary"),
        name=name,
    )(x, w)


class _InCols(NamedTuple):
    q_lat: int
    kv_lat: int
    q_b: int
    k_b: int
    v_b: int
    u_c: int
    k_pe: int
    k_pe_rot: int
    width: int


def _in_cols(cfg):
    offs, acc = [], 0
    for w in (cfg.q_lora, cfg.kv_lora, cfg.h_b * cfg.hd_b, cfg.kv_b * cfg.hd_b, cfg.kv_b * cfg.hd_b,
              2 * cfg.c_conv, cfg.rope_a, cfg.rope_a):
        offs.append(acc)
        acc += w
    width = -(-acc // 512) * 512
    return _InCols(*offs, width)


def _rotate_half_cols(w):
    half = w.shape[-1] // 2
    return jnp.concatenate([-w[..., half:], w[..., :half]], axis=-1)


def _reorder_w_in(w_in, cfg):
    ql, kvl, r = cfg.q_lora, cfg.kv_lora, cfg.rope_a
    o = ql + kvl
    k_pe = w_in[..., o:o + r]
    parts = [w_in[..., :o], w_in[..., o + r:], k_pe, _rotate_half_cols(k_pe)]
    cols = _in_cols(cfg)
    pad = cols.width - (w_in.shape[-1] + r)
    if pad:
        parts.append(jnp.zeros(w_in.shape[:-1] + (pad,), w_in.dtype))
    return jnp.concatenate(parts, axis=-1).astype(_BF16)


def _reorder_w_uq(w_uq, cfg):
    depth, ql, _ = w_uq.shape
    w = w_uq.reshape(depth, ql, cfg.h_a, cfg.nope_a + cfg.rope_a)
    nope = w[..., :cfg.nope_a].reshape(depth, ql, -1)
    pe = w[..., cfg.nope_a:]
    rot = _rotate_half_cols(pe).reshape(depth, ql, -1)
    return jnp.concatenate([nope, pe.reshape(depth, ql, -1), rot], axis=-1).astype(_BF16)


def _post_kernel(p_ref, wuq_ref, gq_ref, gkv_ref, gqb_ref, gkb_ref, ca_ref, sa_ref, cb_ref, sb_ref,
                 qa_ref, ckv_ref, kpe_ref, qb_ref, kb_ref, vb_ref, kb16_ref, vb16_ref, glu_ref, *, cfg, cols):
    ra, hd = cfg.rope_a, cfg.hd_b
    ca, sa, cb, sb = ca_ref[...], sa_ref[...], cb_ref[...], sb_ref[...]

    qn = _rms(p_ref[:, cols.q_lat:cols.q_lat + cfg.q_lora], gq_ref[...]).astype(_BF16)
    qa = jnp.dot(qn, wuq_ref[...], preferred_element_type=_F32)
    pe0 = cfg.h_a * cfg.nope_a
    rot0 = pe0 + cfg.h_a * ra
    scale_a = _LOG2E * (cfg.nope_a + ra) ** -0.5
    for h in range(cfg.h_a):
        nope = qa[:, h * cfg.nope_a:(h + 1) * cfg.nope_a]
        pe = qa[:, pe0 + h * ra:pe0 + (h + 1) * ra] * ca + qa[:, rot0 + h * ra:rot0 + (h + 1) * ra] * sa
        qa_ref[h] = (jnp.concatenate([nope, pe], axis=-1) * scale_a).astype(qa_ref.dtype)

    ckv_ref[...] = _rms(p_ref[:, cols.kv_lat:cols.kv_lat + cfg.kv_lora], gkv_ref[...])
    kpe_ref[...] = p_ref[:, cols.k_pe:cols.k_pe + ra] * ca + p_ref[:, cols.k_pe_rot:cols.k_pe_rot + ra] * sa

    def head_norm_rope(x, g):
        y = _rms(x, g)
        return y * cb + pltpu.roll(y, hd // 2, axis=1) * sb

    scale_b = _LOG2E * hd ** -0.5
    for h in range(cfg.h_b):
        x = p_ref[:, cols.q_b + h * hd:cols.q_b + (h + 1) * hd]
        qb_ref[h] = (head_norm_rope(x, gqb_ref[...]) * scale_b).astype(qb_ref.dtype)
    for h in range(cfg.kv_b):
        x = p_ref[:, cols.k_b + h * hd:cols.k_b + (h + 1) * hd]
        kh = head_norm_rope(x, gkb_ref[...])
        kb_ref[:, h * hd:(h + 1) * hd] = kh
        kb16_ref[:, h * hd:(h + 1) * hd] = kh.astype(kb16_ref.dtype)
    vb = p_ref[:, cols.v_b:cols.v_b + cfg.kv_b * hd]
    vb_ref[...] = vb
    vb16_ref[...] = vb.astype(vb16_ref.dtype)

    a = p_ref[:, cols.u_c:cols.u_c + cfg.c_conv]
    gate = p_ref[:, cols.u_c + cfg.c_conv:cols.u_c + 2 * cfg.c_conv]
    glu_ref[...] = a * jax.nn.sigmoid(gate)


def _post_projection(p, w_uq_r, gains, l, tabs, cfg):
    n = cfg.n
    cols = _in_cols(cfg)
    bm = _pick(n, (256, 128))
    dk_a = cfg.nope_a + cfg.rope_a
    kvw = cfg.kv_b * cfg.hd_b
    row = lambda w: pl.BlockSpec((bm, w), lambda i: (i, 0))
    out_shape = (
        jax.ShapeDtypeStruct((cfg.h_a, n, dk_a), _BF16),
        jax.ShapeDtypeStruct((n, cfg.kv_lora), _F32),
        jax.ShapeDtypeStruct((n, cfg.rope_a), _F32),
        jax.ShapeDtypeStruct((cfg.h_b, n, cfg.hd_b), _BF16),
        jax.ShapeDtypeStruct((n, kvw), _F32),
        jax.ShapeDtypeStruct((n, kvw), _F32),
        jax.ShapeDtypeStruct((n, kvw), _BF16),
        jax.ShapeDtypeStruct((n, kvw), _BF16),
        jax.ShapeDtypeStruct((n, cfg.c_conv), _F32),
    )
    out_specs = (
        pl.BlockSpec((cfg.h_a, bm, dk_a), lambda i: (0, i, 0)),
        row(cfg.kv_lora),
        row(cfg.rope_a),
        pl.BlockSpec((cfg.h_b, bm, cfg.hd_b), lambda i: (0, i, 0)),
        row(kvw),
        row(kvw),
        row(kvw),
        row(kvw),
        row(cfg.c_conv),
    )
    return pl.pallas_call(
        functools.partial(_post_kernel, cfg=cfg, cols=cols),
        out_shape=out_shape,
        grid=(n // bm,),
        in_specs=[row(cols.width), _layer_spec(w_uq_r, l)] + [_layer_spec(g, l) for g in gains]
        + [row(cfg.rope_a), row(cfg.rope_a), row(cfg.hd_b), row(cfg.hd_b)],
        out_specs=out_specs,
        compiler_params=_cparams("arbitrary"),
        name="post_projection",
    )(p, w_uq_r, *gains, *tabs)


def _kv_a_kernel(ckv_ref, kpe_ref, w_ref, k_ref, v_ref, *, cfg):
    kv = jnp.dot(ckv_ref[...].astype(_BF16), w_ref[...], preferred_element_type=_F32)
    kpe = kpe_ref[...]
    hw = cfg.nope_a + cfg.v_a
    for h in range(cfg.h_a):
        k_ref[h] = jnp.concatenate([kv[:, h * hw:h * hw + cfg.nope_a], kpe], axis=-1).astype(k_ref.dtype)
        v_ref[h] = kv[:, h * hw + cfg.nope_a:(h + 1) * hw].astype(v_ref.dtype)


def _kv_a(ckv_keys, kpe_keys, w_ukv, l, cfg):
    t = ckv_keys.shape[0]
    bm = _pick(t, (256, 128))
    dk = cfg.nope_a + cfg.rope_a
    return pl.pallas_call(
        functools.partial(_kv_a_kernel, cfg=cfg),
        out_shape=(jax.ShapeDtypeStruct((cfg.h_a, t, dk), _BF16),
                   jax.ShapeDtypeStruct((cfg.h_a, t, cfg.v_a), _BF16)),
        grid=(t // bm,),
        in_specs=[
            pl.BlockSpec((bm, cfg.kv_lora), lambda i: (i, 0)),
            pl.BlockSpec((bm, cfg.rope_a), lambda i: (i, 0)),
            _layer_spec(w_ukv, l),
        ],
        out_specs=(pl.BlockSpec((cfg.h_a, bm, dk), lambda i: (0, i, 0)),
                   pl.BlockSpec((cfg.h_a, bm, cfg.v_a), lambda i: (0, i, 0))),
        compiler_params=_cparams("arbitrary"),
        name="kv_a",
    )(ckv_keys, kpe_keys, w_ukv)


def _attn_kernel(q_ref, k_ref, v_ref, o_ref, *, hb, group, dv, kv_head_major):
    for j in range(hb):
        kvj = j // group
        q = q_ref[j]
        if kv_head_major:
            k, v = k_ref[kvj], v_ref[kvj]
        else:
            k, v = k_ref[:, kvj * dv:(kvj + 1) * dv], v_ref[:, kvj * dv:(kvj + 1) * dv]
        s = lax.dot_general(q, k, (((1,), (1,)), ((), ())), preferred_element_type=_F32)
        m = jnp.max(s, axis=-1, keepdims=True)
        p = jnp.exp2(s - m)
        l = jnp.sum(p, axis=-1, keepdims=True)
        o = jnp.dot(p.astype(_BF16), v, preferred_element_type=_F32) / l
        o_ref[:, j * dv:(j + 1) * dv] = o.astype(o_ref.dtype)


def _attention(q, k, v, *, n_seq, tq, tk, q_row0, k_row0, hb, group, dv, bq, kv_head_major, name):
    hq, _, dk = q.shape
    hkb = hb // group
    nq = tq // bq
    qb0, kb0 = q_row0 // bq, k_row0 // tk
    assert q_row0 % bq == 0 and k_row0 % tk == 0 and hq % hb == 0 and hb % group == 0
    if kv_head_major:
        k_spec = pl.BlockSpec((hkb, tk, k.shape[2]), lambda s, h, i: (h, kb0 + s, 0))
        v_spec = pl.BlockSpec((hkb, tk, v.shape[2]), lambda s, h, i: (h, kb0 + s, 0))
    else:
        k_spec = pl.BlockSpec((tk, hkb * dv), lambda s, h, i: (kb0 + s, h))
        v_spec = pl.BlockSpec((tk, hkb * dv), lambda s, h, i: (kb0 + s, h))
    return pl.pallas_call(
        functools.partial(_attn_kernel, hb=hb, group=group, dv=dv, kv_head_major=kv_head_major),
        out_shape=jax.ShapeDtypeStruct((n_seq * tq, hq * dv), _BF16),
        grid=(n_seq, hq // hb, nq),
        in_specs=[pl.BlockSpec((hb, bq, dk), lambda s, h, i: (h, qb0 + s * nq + i, 0)), k_spec, v_spec],
        out_specs=pl.BlockSpec((bq, hb * dv), lambda s, h, i: (s * nq + i, h)),
        compiler_params=_cparams("arbitrary", "arbitrary", "arbitrary"),
        name=name,
    )(q, k, v)


_HALO = 16
_CONV_ROWS = 32


def _conv_kernel(prev_ref, cur_ref, next_ref, w_ref, bdw_ref, g_ref, b_ref, o_ref, buf_ref, sh_ref, *, cfg, bt):
    i = pl.program_id(0)
    n_pt = cfg.n_p // bt
    per_seq = cfg.s_s // bt
    j = jnp.maximum(i - n_pt, 0) % per_seq
    lat = i >= n_pt
    p_per = cfg.s_p // bt
    jp = i % p_per
    has_prev = jnp.where(lat, j > 0, jp > 0)
    has_next = jnp.where(lat, j < per_seq - 1, jp < p_per - 1)
    buf_ref[0:_HALO, :] = jnp.where(has_prev, prev_ref[...], 0.0)
    buf_ref[_HALO:_HALO + bt, :] = cur_ref[...]
    buf_ref[_HALO + bt:2 * _HALO + bt, :] = jnp.where(has_next, next_ref[...], 0.0)
    span = sh_ref.shape[1]
    for s in range(_SUBLANES):
        sh_ref[s] = buf_ref[s:s + span, :]
    half = cfg.conv_w // 2
    for r in range(0, bt, _CONV_ROWS):
        acc = jnp.zeros((_CONV_ROWS, cfg.c_conv), _F32)
        for t in range(cfg.conv_w):
            lo = _HALO + r + t - half
            s = lo % _SUBLANES
            acc = acc + sh_ref[s, lo - s:lo - s + _CONV_ROWS, :] * w_ref[t:t + 1, :]
        y = acc + bdw_ref[...]
        mu = jnp.mean(y, axis=-1, keepdims=True)
        yc = y - mu
        var = jnp.mean(yc * yc, axis=-1, keepdims=True)
        z = yc * lax.rsqrt(var + _EPS) * g_ref[...] + b_ref[...]
        o_ref[r:r + _CONV_ROWS, :] = (z * jax.nn.sigmoid(z)).astype(o_ref.dtype)


def _conv_module(v, w_dw, b_dw, g_ln, b_ln, l, cfg):
    n, c = v.shape
    bt = _pick(cfg.s_p, (256, 128))
    assert cfg.s_s % bt == 0 and cfg.conv_w // 2 < _HALO and bt % _CONV_ROWS == 0
    hb = bt // _HALO
    last = n // _HALO - 1
    span = bt + 2 * _HALO - _SUBLANES
    return pl.pallas_call(
        functools.partial(_conv_kernel, cfg=cfg, bt=bt),
        out_shape=jax.ShapeDtypeStruct((n, c), _BF16),
        grid=(n // bt,),
        in_specs=[
            pl.BlockSpec((_HALO, c), lambda i: (jnp.maximum(i * hb - 1, 0), 0)),
            pl.BlockSpec((bt, c), lambda i: (i, 0)),
            pl.BlockSpec((_HALO, c), lambda i: (jnp.minimum((i + 1) * hb, last), 0)),
            _layer_spec(w_dw, l), _layer_spec(b_dw, l), _layer_spec(g_ln, l), _layer_spec(b_ln, l),
        ],
        out_specs=pl.BlockSpec((bt, c), lambda i: (i, 0)),
        scratch_shapes=[pltpu.VMEM((bt + 2 * _HALO, c), _F32), pltpu.VMEM((_SUBLANES, span, c), _F32)],
        compiler_params=_cparams("arbitrary"),
        name="conv_module",
    )(v, v, v, w_dw, b_dw, g_ln, b_ln)


def _out_proj_kernel(a_ref, b_ref, c_ref, w_ref, x_ref, g_ref, o_ref, wb_ref, *, wa, wbw):
    @pl.when(pl.program_id(1) == 0)
    def _():
        wb_ref[...] = w_ref[...].astype(_BF16)

    acc = jnp.dot(a_ref[...], wb_ref[0:wa, :], preferred_element_type=_F32)
    acc = acc + jnp.dot(b_ref[...], wb_ref[wa:wa + wbw, :], preferred_element_type=_F32)
    acc = acc + jnp.dot(c_ref[...], wb_ref[wa + wbw:, :], preferred_element_type=_F32)
    o_ref[...] = x_ref[...] + g_ref[0] * acc


def _out_proj(out_a, out_b, out_c, w_out, x, mod, l, cfg):
    n, d = x.shape
    wa, wbw, wc = out_a.shape[1], out_b.shape[1], out_c.shape[1]
    bm = _row_tile(cfg, (1024, 512, 256, 128))
    bn = _pick(d, (512, 256, 128))
    return pl.pallas_call(
        functools.partial(_out_proj_kernel, wa=wa, wbw=wbw),
        out_shape=jax.ShapeDtypeStruct((n, d), _F32),
        grid=(d // bn, n // bm),
        in_specs=[
            pl.BlockSpec((bm, wa), lambda j, i: (i, 0)),
            pl.BlockSpec((bm, wbw), lambda j, i: (i, 0)),
            pl.BlockSpec((bm, wc), lambda j, i: (i, 0)),
            pl.BlockSpec((None, d, bn), lambda j, i: (l, 0, j)),
            pl.BlockSpec((bm, bn), lambda j, i: (i, j)),
            _mod_spec(l, _GATE1, bn, lambda j, i: _group_of_tile(i, bm, cfg), lambda j, i: j),
        ],
        out_specs=pl.BlockSpec((bm, bn), lambda j, i: (i, j)),
        scratch_shapes=[pltpu.VMEM((d, bn), _BF16)],
        compiler_params=_cparams("arbitrary", "arbitrary"),
        name="out_proj",
    )(out_a, out_b, out_c, w_out, x, mod)


def _norm_router_kernel(x_ref, g_ref, sc_ref, sh_ref, wr_ref, br_ref, h_ref, idx_ref, wts_ref, *, n_exp):
    h = _rms(x_ref[...], g_ref[...]) * (1.0 + sc_ref[0]) + sh_ref[0]
    h_ref[...] = _pack_bf16_pairs(h)
    logits = jnp.dot(h.astype(_BF16), wr_ref[...], preferred_element_type=_F32)
    scores = jax.nn.sigmoid(logits)
    biased = scores + br_ref[...]
    sc_t = scores.T
    bi_t = biased.T
    epg = n_exp // _N_GROUPS
    row = lambda a, e: a[e:e + 1, :]

    group_scores = []
    for g in range(_N_GROUPS):
        vals = [row(bi_t, g * epg + i) for i in range(epg)]
        best = None
        for i in range(epg):
            for j in range(i + 1, epg):
                pair = vals[i] + vals[j]
                best = pair if best is None else jnp.maximum(best, pair)
        group_scores.append(best)
    sel = jnp.zeros_like(group_scores[0], dtype=jnp.int32)
    best = group_scores[0]
    for g in range(1, _N_GROUPS):
        upd = group_scores[g] > best
        sel = jnp.where(upd, g, sel)
        best = jnp.where(upd, group_scores[g], best)

    def in_group(a, i):
        out = row(a, i)
        for g in range(1, _N_GROUPS):
            out = jnp.where(sel == g, row(a, g * epg + i), out)
        return out

    b = [in_group(bi_t, i) for i in range(epg)]
    u = [in_group(sc_t, i) for i in range(epg)]
    i1, v1, w1 = jnp.zeros_like(sel), b[0], u[0]
    for i in range(1, epg):
        upd = b[i] > v1
        i1, v1, w1 = jnp.where(upd, i, i1), jnp.where(upd, b[i], v1), jnp.where(upd, u[i], w1)
    i2 = jnp.zeros_like(sel)
    v2 = jnp.full_like(v1, -jnp.inf)
    w2 = jnp.zeros_like(w1)
    for i in range(epg):
        upd = jnp.logical_and(i1 != i, b[i] > v2)
        i2, v2, w2 = jnp.where(upd, i, i2), jnp.where(upd, b[i], v2), jnp.where(upd, u[i], w2)
    den = w1 + w2
    idx_ref[0:1, :] = sel * epg + i1
    idx_ref[1:2, :] = sel * epg + i2
    wts_ref[0:1, :] = w1 / den
    wts_ref[1:2, :] = w2 / den


def _norm_router(x, g, mod, l, w_router, router_bias, cfg):
    n, d = x.shape
    bm = _row_tile(cfg, (256, 128))
    e = cfg.n_exp
    wr = jnp.zeros((d, _LANES), _BF16).at[:, :e].set(w_router.astype(_BF16))
    br = jnp.zeros((1, _LANES), _F32).at[0, :e].set(router_bias)
    grp = lambda i: _group_of_tile(i, bm, cfg)
    return pl.pallas_call(
        functools.partial(_norm_router_kernel, n_exp=e),
        out_shape=(jax.ShapeDtypeStruct((n, d // 2), _U32),
                   jax.ShapeDtypeStruct((2, n), jnp.int32),
                   jax.ShapeDtypeStruct((2, n), _F32)),
        grid=(n // bm,),
        in_specs=[
            pl.BlockSpec((bm, d), lambda i: (i, 0)),
            _layer_spec(g, l),
            _mod_spec(l, _SCALE2, d, grp),
            _mod_spec(l, _SHIFT2, d, grp),
            pl.BlockSpec((d, _LANES), lambda i: (0, 0)),
            pl.BlockSpec((1, _LANES), lambda i: (0, 0)),
        ],
        out_specs=(pl.BlockSpec((bm, d // 2), lambda i: (i, 0)),
                   pl.BlockSpec((2, bm), lambda i: (0, i)),
                   pl.BlockSpec((2, bm), lambda i: (0, i))),
        compiler_params=_cparams("arbitrary"),
        name="norm_router",
    )(x, g, mod, mod, wr, br)


class _Plan(NamedTuple):
    pos: jax.Array
    pad_lo: jax.Array
    pad_hi: jax.Array
    tile_e: jax.Array
    tile_new: jax.Array
    tile_ok: jax.Array


def _route_plan(idx, n_exp, tm):
    n = idx.shape[1]
    e_flat = idx.reshape(-1)
    onehot = (e_flat[:, None] == jnp.arange(n_exp, dtype=jnp.int32)[None, :]).astype(jnp.int32)
    rank = jnp.sum((jnp.cumsum(onehot, axis=0) - onehot) * onehot, axis=1)
    counts = jnp.sum(onehot, axis=0)
    tiles_per = (counts + tm - 1) // tm
    tile_end = jnp.cumsum(tiles_per)
    tile_start = tile_end - tiles_per
    pos = tile_start[e_flat] * tm + rank
    n_tiles = (2 * n) // tm + n_exp
    tid = jnp.arange(n_tiles, dtype=jnp.int32)
    total = tile_end[-1]
    ok = tid < total
    te = jnp.searchsorted(tile_end, jnp.minimum(tid, total - 1), side="right").astype(jnp.int32)
    new = jnp.concatenate([jnp.ones((1,), jnp.int32), (te[1:] != te[:-1]).astype(jnp.int32)])
    i32 = lambda a: a.astype(jnp.int32)
    return _Plan(i32(pos), i32(tile_start * tm + counts), i32(tile_end * tm), te, new, i32(ok)), n_tiles


_DMA_UNROLL = 8


def _dispatch_kernel(pos_ref, lo_ref, hi_ref, h_hbm, z_hbm, o_hbm, sem, *, n, chunk, n_exp):
    i = pl.program_id(0)
    n_steps = (2 * n) // chunk
    base = i * chunk
    tok0 = base - jnp.where(base >= n, n, 0)

    def row_copy(src, dst_row):
        return pltpu.make_async_copy(src, o_hbm.at[pl.ds(dst_row, 1)], sem)

    def start(r, c):
        row_copy(h_hbm.at[pl.ds(tok0 + r, 1)], pos_ref[base + r]).start()
        return c

    def wait(r, c):
        row_copy(h_hbm.at[pl.ds(0, 1)], 0).wait()
        return c

    lax.fori_loop(0, chunk, start, 0, unroll=_DMA_UNROLL)
    lax.fori_loop(0, chunk, wait, 0, unroll=_DMA_UNROLL)

    def zstart(r, c):
        row_copy(z_hbm.at[pl.ds(0, 1)], r).start()
        return c

    def zwait(r, c):
        row_copy(z_hbm.at[pl.ds(0, 1)], 0).wait()
        return c

    for e in range(n_exp):
        @pl.when(i == e % n_steps)
        def _():
            lax.fori_loop(lo_ref[e], hi_ref[e], zstart, 0)
            lax.fori_loop(lo_ref[e], hi_ref[e], zwait, 0)

    tm = z_hbm.shape[0]

    def tile_copy(t):
        return pltpu.make_async_copy(z_hbm, o_hbm.at[pl.ds(t * tm, tm)], sem)

    def tstart(t, c):
        tile_copy(t).start()
        return c

    def twait(t, c):
        tile_copy(0).wait()
        return c

    @pl.when(i == n_steps - 1)
    def _():
        first_unused = hi_ref[n_exp - 1] // tm
        lax.fori_loop(first_unused, o_hbm.shape[0] // tm, tstart, 0)
        lax.fori_loop(first_unused, o_hbm.shape[0] // tm, twait, 0)


def _dispatch(h_packed, plan, n_rows, tm, cfg):
    n, w = h_packed.shape
    chunk = _pick(n, (512, 256, 128))
    any_spec = pl.BlockSpec(memory_space=pl.ANY)
    return pl.pallas_call(
        functools.partial(_dispatch_kernel, n=n, chunk=chunk, n_exp=cfg.n_exp),
        out_shape=jax.ShapeDtypeStruct((n_rows, w), _U32),
        grid_spec=pltpu.PrefetchScalarGridSpec(
            num_scalar_prefetch=3,
            grid=((2 * n) // chunk,),
            in_specs=[any_spec, any_spec],
            out_specs=any_spec,
            scratch_shapes=[pltpu.SemaphoreType.DMA(())],
        ),
        compiler_params=_cparams("arbitrary"),
        name="moe_dispatch",
    )(plan.pos, plan.pad_lo, plan.pad_hi, h_packed, jnp.zeros((tm, w), _U32))


def _gate_up_kernel(te_ref, new_ref, ok_ref, x_ref, wg_ref, wu_ref, o_ref, wgb_ref, wub_ref):
    t = pl.program_id(1)

    @pl.when(new_ref[t] == 1)
    def _():
        wgb_ref[...] = wg_ref[...].astype(_BF16)
        wub_ref[...] = wu_ref[...].astype(_BF16)

    @pl.when(ok_ref[t] == 1)
    def _():
        lo, hi = _unpack_bf16_pairs(x_ref[...])
        lo, hi = lo.astype(_BF16), hi.astype(_BF16)
        half = lo.shape[1]

        def proj(w_ref):
            return (jnp.dot(lo, w_ref[:half, :], preferred_element_type=_F32)
                    + jnp.dot(hi, w_ref[half:, :], preferred_element_type=_F32))

        a, b = proj(wgb_ref), proj(wub_ref)
        o_ref[...] = (a * jax.nn.sigmoid(a) * b).astype(o_ref.dtype)

    @pl.when(ok_ref[t] == 0)
    def _():
        o_ref[...] = jnp.zeros_like(o_ref)


def _gate_up(xs, w_gate, w_up, l, plan, tm):
    rows, dh = xs.shape
    d, de = w_gate.shape[2:]
    bn = _pick(de, (512, 256, 128))
    n_tiles = rows // tm
    w_spec = pl.BlockSpec((None, None, d, bn), lambda j, t, te, new, ok: (l, te[t], 0, j))
    return pl.pallas_call(
        _gate_up_kernel,
        out_shape=jax.ShapeDtypeStruct((rows, de), _BF16),
        grid_spec=pltpu.PrefetchScalarGridSpec(
            num_scalar_prefetch=3,
            grid=(de // bn, n_tiles),
            in_specs=[pl.BlockSpec((tm, dh), lambda j, t, te, new, ok: (t * ok[t], 0)), w_spec, w_spec],
            out_specs=pl.BlockSpec((tm, bn), lambda j, t, te, new, ok: (t, j)),
            scratch_shapes=[pltpu.VMEM((d, bn), _BF16), pltpu.VMEM((d, bn), _BF16)],
        ),
        compiler_params=_cparams("arbitrary", "arbitrary"),
        name="moe_gate_up",
    )(plan.tile_e, plan.tile_new, plan.tile_ok, xs, w_gate, w_up)


def _down_kernel(te_ref, new_ref, ok_ref, x_ref, w_ref, o_ref, wb_ref):
    t = pl.program_id(1)

    @pl.when(new_ref[t] == 1)
    def _():
        wb_ref[...] = w_ref[...].astype(_BF16)

    @pl.when(ok_ref[t] == 1)
    def _():
        o_ref[...] = _pack_bf16_pairs(jnp.dot(x_ref[...], wb_ref[...], preferred_element_type=_F32))

    @pl.when(ok_ref[t] == 0)
    def _():
        o_ref[...] = jnp.zeros_like(o_ref)


def _down_block(d):
    return _pick(d, (2048, 1024, 512, 256))


def _down(hm, w_down, l, plan, tm):
    rows, de = hm.shape
    d = w_down.shape[3]
    bn = _down_block(d)
    n_tiles = rows // tm
    return pl.pallas_call(
        _down_kernel,
        out_shape=jax.ShapeDtypeStruct((rows, d // 2), _U32),
        grid_spec=pltpu.PrefetchScalarGridSpec(
            num_scalar_prefetch=3,
            grid=(d // bn, n_tiles),
            in_specs=[
                pl.BlockSpec((tm, de), lambda j, t, te, new, ok: (t, 0)),
                pl.BlockSpec((None, None, de, bn), lambda j, t, te, new, ok: (l, te[t], 0, j)),
            ],
            out_specs=pl.BlockSpec((tm, bn // 2), lambda j, t, te, new, ok: (t, j)),
            scratch_shapes=[pltpu.VMEM((de, bn), _BF16)],
        ),
        compiler_params=_cparams("arbitrary", "arbitrary"),
        name="moe_down",
    )(plan.tile_e, plan.tile_new, plan.tile_ok, hm, w_down)


def _combine_kernel(pos_ref, y_hbm, x_ref, g_ref, w_ref, o_ref, buf_ref, sem, *, bt, n, bn):
    base = pl.program_id(0) * bt

    def row_copy(k, r, p):
        return pltpu.make_async_copy(y_hbm.at[pl.ds(p, 1)], buf_ref.at[k, pl.ds(r, 1)], sem)

    def start(r, c):
        row_copy(0, r, pos_ref[base + r]).start()
        row_copy(1, r, pos_ref[n + base + r]).start()
        return c

    def wait(r, c):
        row_copy(0, r, 0).wait()
        row_copy(1, r, 0).wait()
        return c

    lax.fori_loop(0, bt, start, 0, unroll=_DMA_UNROLL)
    lax.fori_loop(0, bt, wait, 0, unroll=_DMA_UNROLL)
    w0, w1 = w_ref[:, 0:1], w_ref[:, 1:2]
    hb = bn // 2
    for j in range(x_ref.shape[1] // bn):
        lo0, hi0 = _unpack_bf16_pairs(buf_ref[0, :, j * hb:(j + 1) * hb])
        lo1, hi1 = _unpack_bf16_pairs(buf_ref[1, :, j * hb:(j + 1) * hb])
        c0, c1, c2 = j * bn, j * bn + hb, (j + 1) * bn
        o_ref[:, c0:c1] = x_ref[:, c0:c1] + g_ref[0][:, c0:c1] * (w0 * lo0 + w1 * lo1)
        o_ref[:, c1:c2] = x_ref[:, c1:c2] + g_ref[0][:, c1:c2] * (w0 * hi0 + w1 * hi1)


def _combine(y, pos, wts_t, x, mod, l, cfg):
    n, d = x.shape
    bt = _row_tile(cfg, (256, 128))
    return pl.pallas_call(
        functools.partial(_combine_kernel, bt=bt, n=n, bn=_down_block(d)),
        out_shape=jax.ShapeDtypeStruct((n, d), _F32),
        grid_spec=pltpu.PrefetchScalarGridSpec(
            num_scalar_prefetch=1,
            grid=(n // bt,),
            in_specs=[
                pl.BlockSpec(memory_space=pl.ANY),
                pl.BlockSpec((bt, d), lambda i, pos: (i, 0)),
                _mod_spec(l, _GATE2, d, lambda i, pos: _group_of_tile(i, bt, cfg)),
                pl.BlockSpec((bt, 2), lambda i, pos: (i, 0)),
            ],
            out_specs=pl.BlockSpec((bt, d), lambda i, pos: (i, 0)),
            scratch_shapes=[pltpu.VMEM((2, bt, d // 2), _U32), pltpu.SemaphoreType.DMA(())],
        ),
        compiler_params=_cparams("arbitrary"),
        name="moe_combine",
    )(pos, y, x, mod, wts_t)


def _rope_tables(cfg):
    t = cfg.s_s
    pos = jnp.arange(t, dtype=jnp.int32)
    row = (pos // _GRID_W).astype(_F32)
    col = (pos % _GRID_W).astype(_F32)

    def angles(rot_dim):
        quarter = rot_dim // 4
        inv_freq = _ROPE_THETA ** (-jnp.arange(quarter, dtype=_F32) / quarter)
        return jnp.concatenate([row[:, None] * inv_freq, col[:, None] * inv_freq], axis=-1)

    def stack(prompt_val, lat):
        lat = jnp.tile(lat, (cfg.b_s, 1))
        return jnp.concatenate([jnp.full((cfg.n_p, lat.shape[1]), prompt_val, _F32), lat], axis=0)

    ang_a, ang_b = angles(cfg.rope_a), angles(cfg.hd_b)
    cos_a, sin_a = jnp.cos(ang_a), jnp.sin(ang_a)
    cos_b, sin_b = jnp.cos(ang_b), jnp.sin(ang_b)
    return (stack(1.0, jnp.concatenate([cos_a, cos_a], -1)), stack(0.0, jnp.concatenate([sin_a, sin_a], -1)),
            stack(1.0, jnp.concatenate([cos_b, cos_b], -1)), stack(0.0, jnp.concatenate([-sin_b, sin_b], -1)))


def _keys_with_cache(own, cache, cfg):
    parts = []
    for b in range(cfg.b_s):
        parts += [cache[b].astype(own.dtype), own[cfg.n_p + b * cfg.s_s:cfg.n_p + (b + 1) * cfg.s_s]]
    parts.append(own[:cfg.n_p])
    return jnp.concatenate(parts, axis=0)


def _make_cfg(x_prompt, x_sample, cache_mla_ckv, cache_mla_krope, cache_gqa_k, w_ada, w_in, g_q_lat,
              w_uq, w_ukv, w_dw, w_router, w_gate):
    b, s, d = x_prompt.shape
    b_s, s_s, _ = x_sample.shape
    past, kv_lora = cache_mla_ckv.shape[2:]
    rope_a = cache_mla_krope.shape[3]
    kv_b, hd_b = cache_gqa_k.shape[3:]
    q_lora = g_q_lat.shape[1]
    conv_w, c_conv = w_dw.shape[1:]
    h_b = (w_in.shape[2] - q_lora - kv_lora - rope_a - 2 * kv_b * hd_b - 2 * c_conv) // hd_b
    hv = d - h_b * hd_b - c_conv
    h_a = (w_uq.shape[2] - w_ukv.shape[2] + hv) // rope_a
    v_a = hv // h_a
    nope_a = w_ukv.shape[2] // h_a - v_a
    return _Cfg(d=d, n_p=b * s, s_p=s, b_s=b_s, s_s=s_s, past=past, depth=w_ada.shape[0], q_lora=q_lora,
                kv_lora=kv_lora, rope_a=rope_a, nope_a=nope_a, v_a=v_a, h_a=h_a, h_b=h_b, kv_b=kv_b,
                hd_b=hd_b, c_conv=c_conv, conv_w=conv_w, n_exp=w_router.shape[1], d_exp=w_gate.shape[3])


def kernel(x_prompt, x_sample, c, c_ctx, cache_mla_ckv, cache_mla_krope, cache_gqa_k, cache_gqa_v, w_ada, b_ada, g_norm1, g_norm2, w_in, g_q_lat, g_kv_lat, w_uq, w_ukv, g_q_b, g_k_b, w_dw, b_dw, g_conv_ln, b_conv_ln, w_out, w_router, router_bias, w_gate, w_up, w_down, g_final):
    cfg = _make_cfg(x_prompt, x_sample, cache_mla_ckv, cache_mla_krope, cache_gqa_k, w_ada, w_in, g_q_lat,
                    w_uq, w_ukv, w_dw, w_router, w_gate)
    d, n_p, depth = cfg.d, cfg.n_p, cfg.depth
    batch, seq = x_prompt.shape[:2]
    kvw = cfg.kv_b * cfg.hd_b
    assert cfg.v_a == cfg.hd_b and 1 + cfg.b_s <= _MOD_ROWS

    x = jnp.concatenate([x_prompt.reshape(n_p, d), x_sample.reshape(cfg.n_s, d)], axis=0)

    m = jnp.concatenate([c_ctx[None, :], c, jnp.zeros((_MOD_ROWS - 1 - cfg.b_s, d), _F32)], axis=0)
    mod = _ada(jax.nn.silu(m).astype(_BF16), w_ada, b_ada).reshape(depth, _MOD_ROWS, 6, 1, d)

    vec = lambda a: a.reshape(depth, 1, a.shape[-1])
    g_norm1, g_norm2 = vec(g_norm1), vec(g_norm2)
    gains = [vec(g_q_lat), vec(g_kv_lat), vec(g_q_b), vec(g_k_b)]
    b_dw, g_conv_ln, b_conv_ln = vec(b_dw), vec(g_conv_ln), vec(b_conv_ln)
    w_in_r = _reorder_w_in(w_in, cfg)
    w_uq_r = _reorder_w_uq(w_uq, cfg)
    w_ukv_b = w_ukv.astype(_BF16)
    tabs = _rope_tables(cfg)
    tm = _pick(2 * cfg.n, (512, 256, 128))
    bq_s = _pick(cfg.s_s, (256, 128))

    ckvs, kpes, kbs, vbs = [], [], [], []
    for l in range(depth):
        h = _norm_mod(x, g_norm1, mod, l, cfg)
        p = _matmul(h, w_in_r, l, _F32, "in_proj")
        q_a, ckv, kpe, q_b, k_b, v_b, k_b16, v_b16, glu = _post_projection(p, w_uq_r, gains, l, tabs, cfg)
        ckvs.append(ckv[:n_p])
        kpes.append(kpe[:n_p])
        kbs.append(k_b[:n_p])
        vbs.append(v_b[:n_p])

        k_a, v_a = _kv_a(_keys_with_cache(ckv, cache_mla_ckv[:, l], cfg),
                         _keys_with_cache(kpe, cache_mla_krope[:, l], cfg), w_ukv_b, l, cfg)
        common_a = dict(group=1, dv=cfg.v_a, kv_head_major=True)
        a_p = _attention(q_a, k_a, v_a, n_seq=batch, tq=seq, tk=seq, q_row0=0, k_row0=cfg.b_s * cfg.tk_s,
                         hb=cfg.h_a, bq=seq, name="attn_a_prompt", **common_a)
        a_s = _attention(q_a, k_a, v_a, n_seq=cfg.b_s, tq=cfg.s_s, tk=cfg.tk_s, q_row0=n_p, k_row0=0,
                         hb=2 if cfg.h_a % 2 == 0 else 1, bq=bq_s, name="attn_a_latent", **common_a)
        k_bk = _keys_with_cache(k_b16, cache_gqa_k[:, l].reshape(cfg.b_s, cfg.past, kvw), cfg)
        v_bk = _keys_with_cache(v_b16, cache_gqa_v[:, l].reshape(cfg.b_s, cfg.past, kvw), cfg)
        grp = cfg.h_b // cfg.kv_b
        common_b = dict(group=grp, dv=cfg.hd_b, kv_head_major=False)
        b_p = _attention(q_b, k_bk, v_bk, n_seq=batch, tq=seq, tk=seq, q_row0=0, k_row0=cfg.b_s * cfg.tk_s,
                         hb=cfg.h_b, bq=seq, name="attn_b_prompt", **common_b)
        b_s = _attention(q_b, k_bk, v_bk, n_seq=cfg.b_s, tq=cfg.s_s, tk=cfg.tk_s, q_row0=n_p, k_row0=0,
                         hb=grp, bq=bq_s, name="attn_b_latent", **common_b)
        out_c = _conv_module(glu, w_dw, b_dw, g_conv_ln, b_conv_ln, l, cfg)

        x = _out_proj(jnp.concatenate([a_p, a_s], axis=0), jnp.concatenate([b_p, b_s], axis=0), out_c,
                      w_out, x, mod, l, cfg)

        h2, idx, wts = _norm_router(x, g_norm2, mod, l, w_router, router_bias, cfg)
        plan, n_tiles = _route_plan(idx, cfg.n_exp, tm)
        xs = _dispatch(h2, plan, n_tiles * tm, tm, cfg)
        hm = _gate_up(xs, w_gate, w_up, l, plan, tm)
        y = _down(hm, w_down, l, plan, tm)
        x = _combine(y, plan.pos, wts.T, x, mod, l, cfg)

    y_all = _final_norm(x, g_final)
    stack = lambda parts, tail: jnp.stack([a.reshape((batch, seq) + tail) for a in parts], axis=1)
    return (y_all[:n_p].reshape(batch, seq, d),
            y_all[n_p:].reshape(cfg.b_s, cfg.s_s, d),
            stack(ckvs, (cfg.kv_lora,)),
            stack(kpes, (cfg.rope_a,)),
            stack(kbs, (cfg.kv_b, cfg.hd_b)),
            stack(vbs, (cfg.kv_b, cfg.hd_b)))
```

```python
import functools
from typing import NamedTuple

import jax
import jax.numpy as jnp
from jax import lax
from jax.experimental import pallas as pl
from jax.experimental.pallas import tpu as pltpu

_F32 = jnp.float32
_BF16 = jnp.bfloat16
_U32 = jnp.uint32

_GRID_W = 64
_N_GROUPS = 4
_ROPE_THETA = 10000.0
_EPS = 1e-6
_LOG2E = 1.4426950408889634

_V7X_VMEM_BYTES = 64 * 1024 * 1024
_VMEM_LIMIT = _V7X_VMEM_BYTES - 8 * 1024 * 1024
_LANES = 128
_SUBLANES = 8
_MOD_ROWS = 16
_SHIFT1, _SCALE1, _GATE1, _SHIFT2, _SCALE2, _GATE2 = range(6)


class _Cfg(NamedTuple):
    d: int
    n_p: int
    s_p: int
    b_s: int
    s_s: int
    past: int
    depth: int
    q_lora: int
    kv_lora: int
    rope_a: int
    nope_a: int
    v_a: int
    h_a: int
    h_b: int
    kv_b: int
    hd_b: int
    c_conv: int
    conv_w: int
    n_exp: int
    d_exp: int

    @property
    def n_s(self):
        return self.b_s * self.s_s

    @property
    def n(self):
        return self.n_p + self.n_s

    @property
    def tk_s(self):
        return self.past + self.s_s


def _pick(n, candidates):
    for c in candidates:
        if n % c == 0:
            return c
    raise ValueError(f"no tile in {candidates} divides {n}")


def _row_tile(cfg, candidates):
    for c in candidates:
        if cfg.n_p % c == 0 and cfg.s_s % c == 0:
            return c
    raise ValueError(f"no row tile in {candidates}")


def _cparams(*sem):
    return pltpu.CompilerParams(dimension_semantics=sem, vmem_limit_bytes=_VMEM_LIMIT)


def _group_of_tile(i, bm, cfg):
    r0 = i * bm
    return jnp.where(r0 < cfg.n_p, 0, 1 + (r0 - cfg.n_p) // cfg.s_s)


def _mod_spec(l, which, width, group_fn, col_fn=None):
    def index(*ids):
        return (l, group_fn(*ids), which, 0, 0 if col_fn is None else col_fn(*ids))
    return pl.BlockSpec((None, 1, None, 1, width), index)


def _layer_spec(a, l):
    nd = a.ndim - 1
    return pl.BlockSpec((None,) + a.shape[1:], lambda *ids: (l,) + (0,) * nd)


def _pack_bf16_pairs(x):
    w = x.shape[1] // 2
    r = x.astype(_BF16).astype(_F32)
    lo = lax.bitcast_convert_type(r[:, :w], _U32) >> 16
    hi = lax.bitcast_convert_type(r[:, w:], _U32) & _U32(0xFFFF0000)
    return lo | hi


def _ones_block(rows, width, dtype):
    col = lax.broadcasted_iota(jnp.int32, (rows, width), 1)
    return jnp.where(col == 0, 1.0, 0.0).astype(dtype)


def _unpack_bf16_pairs(p):
    lo = lax.bitcast_convert_type(p << 16, _F32)
    hi = lax.bitcast_convert_type(p & _U32(0xFFFF0000), _F32)
    return lo, hi


def _store_slabs(o_ref, x):
    rows, w = x.shape
    k = w // _LANES
    for s in range(k):
        o_ref[pl.ds(s, rows, stride=k), :] = x[:, s * _LANES:(s + 1) * _LANES]


def _load_slab_col(ref, s, rows, k):
    return ref[pl.ds(s, rows, stride=k), :]


def _ada_kernel(s_ref, w_ref, b_ref, o_ref):
    w = w_ref[0].astype(_BF16)
    o_ref[0] = jnp.dot(s_ref[...], w, preferred_element_type=_F32) + b_ref[0]


def _ada(sm, w_ada, b_ada):
    depth, d, n6 = w_ada.shape
    bn = _pick(n6, (512, 256, 128))
    return pl.pallas_call(
        _ada_kernel,
        out_shape=jax.ShapeDtypeStruct((depth, _MOD_ROWS, n6), _F32),
        grid=(depth, n6 // bn),
        in_specs=[
            pl.BlockSpec((_MOD_ROWS, d), lambda l, j: (0, 0)),
            pl.BlockSpec((1, d, bn), lambda l, j: (l, 0, j)),
            pl.BlockSpec((1, 1, bn), lambda l, j: (l, 0, j)),
        ],
        out_specs=pl.BlockSpec((1, _MOD_ROWS, bn), lambda l, j: (l, 0, j)),
        compiler_params=_cparams("arbitrary", "arbitrary"),
        name="ada",
    )(sm, w_ada, b_ada.reshape(depth, 1, n6))


def _rms(x, g):
    return x * lax.rsqrt(jnp.mean(x * x, axis=-1, keepdims=True) + _EPS) * g


def _norm_mod_kernel(x_ref, g_ref, sc_ref, sh_ref, o_ref):
    h = _rms(x_ref[...], g_ref[...]) * (1.0 + sc_ref[0]) + sh_ref[0]
    o_ref[...] = h.astype(o_ref.dtype)


def _norm_mod(x, g, mod, l, cfg):
    n, d = x.shape
    bm = _row_tile(cfg, (256, 128))
    grp = lambda i: _group_of_tile(i, bm, cfg)
    return pl.pallas_call(
        _norm_mod_kernel,
        out_shape=jax.ShapeDtypeStruct((n, d), _BF16),
        grid=(n // bm,),
        in_specs=[
            pl.BlockSpec((bm, d), lambda i: (i, 0)),
            _layer_spec(g, l),
            _mod_spec(l, _SCALE1, d, grp),
            _mod_spec(l, _SHIFT1, d, grp),
        ],
        out_specs=pl.BlockSpec((bm, d), lambda i: (i, 0)),
        compiler_params=_cparams("arbitrary"),
        name="norm_mod",
    )(x, g, mod, mod)


def _final_norm_kernel(x_ref, g_ref, o_ref):
    o_ref[...] = _rms(x_ref[...], g_ref[...])


def _final_norm(x, g):
    n, d = x.shape
    bm = _pick(n, (256, 128))
    return pl.pallas_call(
        _final_norm_kernel,
        out_shape=jax.ShapeDtypeStruct((n, d), _F32),
        grid=(n // bm,),
        in_specs=[pl.BlockSpec((bm, d), lambda i: (i, 0)), pl.BlockSpec((1, d), lambda i: (0, 0))],
        out_specs=pl.BlockSpec((bm, d), lambda i: (i, 0)),
        compiler_params=_cparams("arbitrary"),
        name="final_norm",
    )(x, g.reshape(1, d))


def _mm_kernel(x_ref, w_ref, o_ref):
    o_ref[...] = jnp.dot(x_ref[...], w_ref[...], preferred_element_type=_F32).astype(o_ref.dtype)


def _matmul(x, w, l, out_dtype, name):
    m, k = x.shape
    n = w.shape[2]
    bm = _pick(m, (1024, 512, 256, 128))
    bn = _pick(n, (512, 256, 128))
    return pl.pallas_call(
        _mm_kernel,
        out_shape=jax.ShapeDtypeStruct((m, n), out_dtype),
        grid=(n // bn, m // bm),
        in_specs=[pl.BlockSpec((bm, k), lambda j, i: (i, 0)),
                  pl.BlockSpec((None, k, bn), lambda j, i: (l, 0, j))],
        out_specs=pl.BlockSpec((bm, bn), lambda j, i: (i, j)),
        compiler_params=_cparams("arbitrary", "arbitrary"),
        name=name,
    )(x, w)


class _InCols(NamedTuple):
    q_lat: int
    kv_lat: int
    q_b: int
    k_b: int
    v_b: int
    u_c: int
    k_pe: int
    k_pe_rot: int
    width: int


def _in_cols(cfg):
    offs, acc = [], 0
    for w in (cfg.q_lora, cfg.kv_lora, cfg.h_b * cfg.hd_b, cfg.kv_b * cfg.hd_b, cfg.kv_b * cfg.hd_b,
              2 * cfg.c_conv, cfg.rope_a, cfg.rope_a):
        offs.append(acc)
        acc += w
    width = -(-acc // 512) * 512
    return _InCols(*offs, width)


def _rotate_half_cols(w):
    half = w.shape[-1] // 2
    return jnp.concatenate([-w[..., half:], w[..., :half]], axis=-1)


def _reorder_w_in(w_in, cfg):
    ql, kvl, r = cfg.q_lora, cfg.kv_lora, cfg.rope_a
    o = ql + kvl
    k_pe = w_in[..., o:o + r]
    parts = [w_in[..., :o], w_in[..., o + r:], k_pe, _rotate_half_cols(k_pe)]
    cols = _in_cols(cfg)
    pad = cols.width - (w_in.shape[-1] + r)
    if pad:
        parts.append(jnp.zeros(w_in.shape[:-1] + (pad,), w_in.dtype))
    return jnp.concatenate(parts, axis=-1).astype(_BF16)


def _reorder_w_uq(w_uq, cfg):
    depth, ql, _ = w_uq.shape
    w = w_uq.reshape(depth, ql, cfg.h_a, cfg.nope_a + cfg.rope_a)
    nope = w[..., :cfg.nope_a].reshape(depth, ql, -1)
    pe = w[..., cfg.nope_a:]
    rot = _rotate_half_cols(pe).reshape(depth, ql, -1)
    return jnp.concatenate([nope, pe.reshape(depth, ql, -1), rot], axis=-1).astype(_BF16)


def _post_kernel(p_ref, wuq_ref, gq_ref, gkv_ref, gqb_ref, gkb_ref, ca_ref, sa_ref, cb_ref, sb_ref,
                 qa_ref, ckv_ref, kpe_ref, qb_ref, kb_ref, vb_ref, kb16_ref, vb16_ref, glu_ref, *, cfg, cols):
    ra, hd = cfg.rope_a, cfg.hd_b
    ca, sa, cb, sb = ca_ref[...], sa_ref[...], cb_ref[...], sb_ref[...]

    qn = _rms(p_ref[:, cols.q_lat:cols.q_lat + cfg.q_lora], gq_ref[...]).astype(_BF16)
    qa = jnp.dot(qn, wuq_ref[...], preferred_element_type=_F32)
    pe0 = cfg.h_a * cfg.nope_a
    rot0 = pe0 + cfg.h_a * ra
    scale_a = _LOG2E * (cfg.nope_a + ra) ** -0.5
    for h in range(cfg.h_a):
        nope = qa[:, h * cfg.nope_a:(h + 1) * cfg.nope_a]
        pe = qa[:, pe0 + h * ra:pe0 + (h + 1) * ra] * ca + qa[:, rot0 + h * ra:rot0 + (h + 1) * ra] * sa
        qa_ref[h] = (jnp.concatenate([nope, pe], axis=-1) * scale_a).astype(qa_ref.dtype)

    ckv_ref[...] = _rms(p_ref[:, cols.kv_lat:cols.kv_lat + cfg.kv_lora], gkv_ref[...])
    kpe_ref[...] = p_ref[:, cols.k_pe:cols.k_pe + ra] * ca + p_ref[:, cols.k_pe_rot:cols.k_pe_rot + ra] * sa

    def head_norm_rope(x, g):
        y = _rms(x, g)
        return y * cb + pltpu.roll(y, hd // 2, axis=1) * sb

    scale_b = _LOG2E * hd ** -0.5
    for h in range(cfg.h_b):
        x = p_ref[:, cols.q_b + h * hd:cols.q_b + (h + 1) * hd]
        qb_ref[h] = (head_norm_rope(x, gqb_ref[...]) * scale_b).astype(qb_ref.dtype)
    for h in range(cfg.kv_b):
        x = p_ref[:, cols.k_b + h * hd:cols.k_b + (h + 1) * hd]
        kh = head_norm_rope(x, gkb_ref[...])
        kb_ref[:, h * hd:(h + 1) * hd] = kh
        kb16_ref[:, h * hd:(h + 1) * hd] = kh.astype(kb16_ref.dtype)
    vb = p_ref[:, cols.v_b:cols.v_b + cfg.kv_b * hd]
    vb_ref[...] = vb
    ones = _ones_block(vb.shape[0], hd, vb16_ref.dtype)
    for h in range(cfg.kv_b):
        vb16_ref[:, 2 * h * hd:(2 * h + 1) * hd] = vb[:, h * hd:(h + 1) * hd].astype(vb16_ref.dtype)
        vb16_ref[:, (2 * h + 1) * hd:(2 * h + 2) * hd] = ones

    a = p_ref[:, cols.u_c:cols.u_c + cfg.c_conv]
    gate = p_ref[:, cols.u_c + cfg.c_conv:cols.u_c + 2 * cfg.c_conv]
    glu_ref[...] = a * jax.nn.sigmoid(gate)


def _post_projection(p, w_uq_r, gains, l, tabs, cfg):
    n = cfg.n
    cols = _in_cols(cfg)
    bm = _pick(n, (256, 128))
    dk_a = cfg.nope_a + cfg.rope_a
    kvw = cfg.kv_b * cfg.hd_b
    row = lambda w: pl.BlockSpec((bm, w), lambda i: (i, 0))
    out_shape = (
        jax.ShapeDtypeStruct((cfg.h_a, n, dk_a), _BF16),
        jax.ShapeDtypeStruct((n, cfg.kv_lora), _F32),
        jax.ShapeDtypeStruct((n, cfg.rope_a), _F32),
        jax.ShapeDtypeStruct((cfg.h_b, n, cfg.hd_b), _BF16),
        jax.ShapeDtypeStruct((n, kvw), _F32),
        jax.ShapeDtypeStruct((n, kvw), _F32),
        jax.ShapeDtypeStruct((n, kvw), _BF16),
        jax.ShapeDtypeStruct((n, 2 * kvw), _BF16),
        jax.ShapeDtypeStruct((n, cfg.c_conv), _F32),
    )
    out_specs = (
        pl.BlockSpec((cfg.h_a, bm, dk_a), lambda i: (0, i, 0)),
        row(cfg.kv_lora),
        row(cfg.rope_a),
        pl.BlockSpec((cfg.h_b, bm, cfg.hd_b), lambda i: (0, i, 0)),
        row(kvw),
        row(kvw),
        row(kvw),
        row(2 * kvw),
        row(cfg.c_conv),
    )
    return pl.pallas_call(
        functools.partial(_post_kernel, cfg=cfg, cols=cols),
        out_shape=out_shape,
        grid=(n // bm,),
        in_specs=[row(cols.width), _layer_spec(w_uq_r, l)] + [_layer_spec(g, l) for g in gains]
        + [row(cfg.rope_a), row(cfg.rope_a), row(cfg.hd_b), row(cfg.hd_b)],
        out_specs=out_specs,
        compiler_params=_cparams("arbitrary"),
        name="post_projection",
    )(p, w_uq_r, *gains, *tabs)


def _kv_a_kernel(ckv_ref, kpe_ref, w_ref, k_ref, v_ref, *, cfg):
    kv = jnp.dot(ckv_ref[...].astype(_BF16), w_ref[...], preferred_element_type=_F32)
    kpe = kpe_ref[...]
    hw = cfg.nope_a + cfg.v_a
    for h in range(cfg.h_a):
        k_ref[h] = jnp.concatenate([kv[:, h * hw:h * hw + cfg.nope_a], kpe], axis=-1).astype(k_ref.dtype)
        v = kv[:, h * hw + cfg.nope_a:(h + 1) * hw].astype(v_ref.dtype)
        v_ref[h] = jnp.concatenate([v, _ones_block(v.shape[0], cfg.v_a, v_ref.dtype)], axis=-1)


def _kv_a(ckv_keys, kpe_keys, w_ukv, l, cfg):
    t = ckv_keys.shape[0]
    bm = _pick(t, (256, 128))
    dk = cfg.nope_a + cfg.rope_a
    return pl.pallas_call(
        functools.partial(_kv_a_kernel, cfg=cfg),
        out_shape=(jax.ShapeDtypeStruct((cfg.h_a, t, dk), _BF16),
                   jax.ShapeDtypeStruct((cfg.h_a, t, 2 * cfg.v_a), _BF16)),
        grid=(t // bm,),
        in_specs=[
            pl.BlockSpec((bm, cfg.kv_lora), lambda i: (i, 0)),
            pl.BlockSpec((bm, cfg.rope_a), lambda i: (i, 0)),
            _layer_spec(w_ukv, l),
        ],
        out_specs=(pl.BlockSpec((cfg.h_a, bm, dk), lambda i: (0, i, 0)),
                   pl.BlockSpec((cfg.h_a, bm, 2 * cfg.v_a), lambda i: (0, i, 0))),
        compiler_params=_cparams("arbitrary"),
        name="kv_a",
    )(ckv_keys, kpe_keys, w_ukv)


def _attn_kernel(q_ref, k_ref, v_ref, o_ref, *, hb, group, dv, kv_head_major):
    for j in range(hb):
        kvj = j // group
        q = q_ref[j]
        if kv_head_major:
            k, v = k_ref[kvj], v_ref[kvj]
        else:
            k, v = k_ref[:, kvj * dv:(kvj + 1) * dv], v_ref[:, 2 * kvj * dv:2 * (kvj + 1) * dv]
        s = lax.dot_general(q, k, (((1,), (1,)), ((), ())), preferred_element_type=_F32)
        m = jnp.max(s, axis=-1, keepdims=True)
        p = jnp.exp2((s - m).astype(_BF16))
        ov = jnp.dot(p, v, preferred_element_type=_F32)
        o = ov[:, :dv] / ov[:, dv:dv + 1]
        o_ref[:, j * dv:(j + 1) * dv] = o.astype(o_ref.dtype)


def _attention(q, k, v, *, n_seq, tq, tk, q_row0, k_row0, hb, group, dv, bq, kv_head_major, name):
    hq, _, dk = q.shape
    hkb = hb // group
    nq = tq // bq
    qb0, kb0 = q_row0 // bq, k_row0 // tk
    assert q_row0 % bq == 0 and k_row0 % tk == 0 and hq % hb == 0 and hb % group == 0
    if kv_head_major:
        k_spec = pl.BlockSpec((hkb, tk, k.shape[2]), lambda s, h, i: (h, kb0 + s, 0))
        v_spec = pl.BlockSpec((hkb, tk, v.shape[2]), lambda s, h, i: (h, kb0 + s, 0))
    else:
        k_spec = pl.BlockSpec((tk, hkb * dv), lambda s, h, i: (kb0 + s, h))
        v_spec = pl.BlockSpec((tk, 2 * hkb * dv), lambda s, h, i: (kb0 + s, h))
    return pl.pallas_call(
        functools.partial(_attn_kernel, hb=hb, group=group, dv=dv, kv_head_major=kv_head_major),
        out_shape=jax.ShapeDtypeStruct((n_seq * tq, hq * dv), _BF16),
        grid=(n_seq, hq // hb, nq),
        in_specs=[pl.BlockSpec((hb, bq, dk), lambda s, h, i: (h, qb0 + s * nq + i, 0)), k_spec, v_spec],
        out_specs=pl.BlockSpec((bq, hb * dv), lambda s, h, i: (s * nq + i, h)),
        compiler_params=_cparams("arbitrary", "arbitrary", "arbitrary"),
        name=name,
    )(q, k, v)


_HALO = 16
_CONV_ROWS = 32


def _conv_kernel(prev_ref, cur_ref, next_ref, w_ref, bdw_ref, g_ref, b_ref, o_ref, buf_ref, sh_ref, *, cfg, bt):
    i = pl.program_id(0)
    n_pt = cfg.n_p // bt
    per_seq = cfg.s_s // bt
    j = jnp.maximum(i - n_pt, 0) % per_seq
    lat = i >= n_pt
    p_per = cfg.s_p // bt
    jp = i % p_per
    has_prev = jnp.where(lat, j > 0, jp > 0)
    has_next = jnp.where(lat, j < per_seq - 1, jp < p_per - 1)
    buf_ref[0:_HALO, :] = jnp.where(has_prev, prev_ref[...], 0.0)
    buf_ref[_HALO:_HALO + bt, :] = cur_ref[...]
    buf_ref[_HALO + bt:2 * _HALO + bt, :] = jnp.where(has_next, next_ref[...], 0.0)
    span = sh_ref.shape[1]
    for s in range(_SUBLANES):
        sh_ref[s] = buf_ref[s:s + span, :]
    half = cfg.conv_w // 2
    for r in range(0, bt, _CONV_ROWS):
        acc = jnp.zeros((_CONV_ROWS, cfg.c_conv), _F32)
        for t in range(cfg.conv_w):
            lo = _HALO + r + t - half
            s = lo % _SUBLANES
            acc = acc + sh_ref[s, lo - s:lo - s + _CONV_ROWS, :] * w_ref[t:t + 1, :]
        y = acc + bdw_ref[...]
        mu = jnp.mean(y, axis=-1, keepdims=True)
        yc = y - mu
        var = jnp.mean(yc * yc, axis=-1, keepdims=True)
        z = yc * lax.rsqrt(var + _EPS) * g_ref[...] + b_ref[...]
        o_ref[r:r + _CONV_ROWS, :] = (z * jax.nn.sigmoid(z)).astype(o_ref.dtype)


def _conv_module(v, w_dw, b_dw, g_ln, b_ln, l, cfg):
    n, c = v.shape
    bt = _pick(cfg.s_p, (256, 128))
    assert cfg.s_s % bt == 0 and cfg.conv_w // 2 < _HALO and bt % _CONV_ROWS == 0
    hb = bt // _HALO
    last = n // _HALO - 1
    span = bt + 2 * _HALO - _SUBLANES
    return pl.pallas_call(
        functools.partial(_conv_kernel, cfg=cfg, bt=bt),
        out_shape=jax.ShapeDtypeStruct((n, c), _BF16),
        grid=(n // bt,),
        in_specs=[
            pl.BlockSpec((_HALO, c), lambda i: (jnp.maximum(i * hb - 1, 0), 0)),
            pl.BlockSpec((bt, c), lambda i: (i, 0)),
            pl.BlockSpec((_HALO, c), lambda i: (jnp.minimum((i + 1) * hb, last), 0)),
            _layer_spec(w_dw, l), _layer_spec(b_dw, l), _layer_spec(g_ln, l), _layer_spec(b_ln, l),
        ],
        out_specs=pl.BlockSpec((bt, c), lambda i: (i, 0)),
        scratch_shapes=[pltpu.VMEM((bt + 2 * _HALO, c), _F32), pltpu.VMEM((_SUBLANES, span, c), _F32)],
        compiler_params=_cparams("arbitrary"),
        name="conv_module",
    )(v, v, v, w_dw, b_dw, g_ln, b_ln)


def _out_proj_kernel(a_ref, b_ref, c_ref, w_ref, x_ref, g_ref, o_ref, wb_ref, *, wa, wbw):
    @pl.when(pl.program_id(1) == 0)
    def _():
        wb_ref[...] = w_ref[...].astype(_BF16)

    acc = jnp.dot(a_ref[...], wb_ref[0:wa, :], preferred_element_type=_F32)
    acc = acc + jnp.dot(b_ref[...], wb_ref[wa:wa + wbw, :], preferred_element_type=_F32)
    acc = acc + jnp.dot(c_ref[...], wb_ref[wa + wbw:, :], preferred_element_type=_F32)
    o_ref[...] = x_ref[...] + g_ref[0] * acc


def _out_proj(out_a, out_b, out_c, w_out, x, mod, l, cfg):
    n, d = x.shape
    wa, wbw, wc = out_a.shape[1], out_b.shape[1], out_c.shape[1]
    bm = _row_tile(cfg, (1024, 512, 256, 128))
    bn = _pick(d, (512, 256, 128))
    return pl.pallas_call(
        functools.partial(_out_proj_kernel, wa=wa, wbw=wbw),
        out_shape=jax.ShapeDtypeStruct((n, d), _F32),
        grid=(d // bn, n // bm),
        in_specs=[
            pl.BlockSpec((bm, wa), lambda j, i: (i, 0)),
            pl.BlockSpec((bm, wbw), lambda j, i: (i, 0)),
            pl.BlockSpec((bm, wc), lambda j, i: (i, 0)),
            pl.BlockSpec((None, d, bn), lambda j, i: (l, 0, j)),
            pl.BlockSpec((bm, bn), lambda j, i: (i, j)),
            _mod_spec(l, _GATE1, bn, lambda j, i: _group_of_tile(i, bm, cfg), lambda j, i: j),
        ],
        out_specs=pl.BlockSpec((bm, bn), lambda j, i: (i, j)),
        scratch_shapes=[pltpu.VMEM((d, bn), _BF16)],
        compiler_params=_cparams("arbitrary", "arbitrary"),
        name="out_proj",
    )(out_a, out_b, out_c, w_out, x, mod)


def _norm_router_kernel(x_ref, g_ref, sc_ref, sh_ref, wr_ref, br_ref, h_ref, idx_ref, wts_ref, *, n_exp):
    h = _rms(x_ref[...], g_ref[...]) * (1.0 + sc_ref[0]) + sh_ref[0]
    _store_slabs(h_ref, _pack_bf16_pairs(h))
    logits = jnp.dot(h.astype(_BF16), wr_ref[...], preferred_element_type=_F32)
    scores = jax.nn.sigmoid(logits)
    biased = scores + br_ref[...]
    sc_t = scores.T
    bi_t = biased.T
    epg = n_exp // _N_GROUPS
    row = lambda a, e: a[e:e + 1, :]

    group_scores = []
    for g in range(_N_GROUPS):
        vals = [row(bi_t, g * epg + i) for i in range(epg)]
        best = None
        for i in range(epg):
            for j in range(i + 1, epg):
                pair = vals[i] + vals[j]
                best = pair if best is None else jnp.maximum(best, pair)
        group_scores.append(best)
    sel = jnp.zeros_like(group_scores[0], dtype=jnp.int32)
    best = group_scores[0]
    for g in range(1, _N_GROUPS):
        upd = group_scores[g] > best
        sel = jnp.where(upd, g, sel)
        best = jnp.where(upd, group_scores[g], best)

    def in_group(a, i):
        out = row(a, i)
        for g in range(1, _N_GROUPS):
            out = jnp.where(sel == g, row(a, g * epg + i), out)
        return out

    b = [in_group(bi_t, i) for i in range(epg)]
    u = [in_group(sc_t, i) for i in range(epg)]
    i1, v1, w1 = jnp.zeros_like(sel), b[0], u[0]
    for i in range(1, epg):
        upd = b[i] > v1
        i1, v1, w1 = jnp.where(upd, i, i1), jnp.where(upd, b[i], v1), jnp.where(upd, u[i], w1)
    i2 = jnp.zeros_like(sel)
    v2 = jnp.full_like(v1, -jnp.inf)
    w2 = jnp.zeros_like(w1)
    for i in range(epg):
        upd = jnp.logical_and(i1 != i, b[i] > v2)
        i2, v2, w2 = jnp.where(upd, i, i2), jnp.where(upd, b[i], v2), jnp.where(upd, u[i], w2)
    den = w1 + w2
    idx_ref[0:1, :] = sel * epg + i1
    idx_ref[1:2, :] = sel * epg + i2
    wts_ref[0:1, :] = w1 / den
    wts_ref[1:2, :] = w2 / den


def _norm_router(x, g, mod, l, w_router, router_bias, cfg):
    n, d = x.shape
    bm = _row_tile(cfg, (256, 128))
    e = cfg.n_exp
    k_slab = d // 2 // _LANES
    wr = jnp.zeros((d, _LANES), _BF16).at[:, :e].set(w_router.astype(_BF16))
    br = jnp.zeros((1, _LANES), _F32).at[0, :e].set(router_bias)
    grp = lambda i: _group_of_tile(i, bm, cfg)
    return pl.pallas_call(
        functools.partial(_norm_router_kernel, n_exp=e),
        out_shape=(jax.ShapeDtypeStruct((n * k_slab, _LANES), _U32),
                   jax.ShapeDtypeStruct((2, n), jnp.int32),
                   jax.ShapeDtypeStruct((2, n), _F32)),
        grid=(n // bm,),
        in_specs=[
            pl.BlockSpec((bm, d), lambda i: (i, 0)),
            _layer_spec(g, l),
            _mod_spec(l, _SCALE2, d, grp),
            _mod_spec(l, _SHIFT2, d, grp),
            pl.BlockSpec((d, _LANES), lambda i: (0, 0)),
            pl.BlockSpec((1, _LANES), lambda i: (0, 0)),
        ],
        out_specs=(pl.BlockSpec((bm * k_slab, _LANES), lambda i: (i, 0)),
                   pl.BlockSpec((2, bm), lambda i: (0, i)),
                   pl.BlockSpec((2, bm), lambda i: (0, i))),
        compiler_params=_cparams("arbitrary"),
        name="norm_router",
    )(x, g, mod, mod, wr, br)


class _Plan(NamedTuple):
    pos: jax.Array
    row_tok: jax.Array
    tile_e: jax.Array
    tile_new: jax.Array
    tile_ok: jax.Array


def _route_plan(idx, n_exp, tm):
    n = idx.shape[1]
    e_flat = idx.reshape(-1)
    onehot = (e_flat[:, None] == jnp.arange(n_exp, dtype=jnp.int32)[None, :]).astype(jnp.int32)
    rank = jnp.sum((jnp.cumsum(onehot, axis=0) - onehot) * onehot, axis=1)
    counts = jnp.sum(onehot, axis=0)
    tiles_per = (counts + tm - 1) // tm
    tile_end = jnp.cumsum(tiles_per)
    tile_start = tile_end - tiles_per
    pos = (tile_start[e_flat] * tm + rank).astype(jnp.int32)
    n_tiles = (2 * n) // tm + n_exp
    tok = jnp.arange(2 * n, dtype=jnp.int32) % n
    row_tok = jnp.zeros((n_tiles * tm,), jnp.int32).at[pos].set(tok)
    tid = jnp.arange(n_tiles, dtype=jnp.int32)
    total = tile_end[-1]
    ok = (tid < total).astype(jnp.int32)
    te = jnp.searchsorted(tile_end, jnp.minimum(tid, total - 1), side="right").astype(jnp.int32)
    new = jnp.concatenate([jnp.ones((1,), jnp.int32), (te[1:] != te[:-1]).astype(jnp.int32)])
    return _Plan(pos, row_tok, te, new, ok), n_tiles


_DMA_UNROLL = 8


def _prefetched_tiles(t, n_t, ok_ref, start_tile, wait_tile, consume):
    slot = t % 2
    nxt = jnp.minimum(t + 1, n_t - 1)
    fetch_next = jnp.logical_and(t + 1 < n_t, ok_ref[nxt] == 1)

    @pl.when(jnp.logical_and(t == 0, ok_ref[0] == 1))
    def _():
        start_tile(0, 0)

    for sl in (0, 1):
        @pl.when(jnp.logical_and(fetch_next, slot == 1 - sl))
        def _():
            start_tile(t + 1, sl)

    for sl in (0, 1):
        @pl.when(jnp.logical_and(ok_ref[t] == 1, slot == sl))
        def _():
            wait_tile(sl)
            consume(sl)


def _dispatch_kernel(tok_ref, ok_ref, h_hbm, o_ref, buf0, buf1, sem, *, tm, k):
    t = pl.program_id(0)
    bufs = (buf0, buf1)

    def slab_copy(slot, r, tok):
        return pltpu.make_async_copy(h_hbm.at[pl.ds(tok * k, k)], bufs[slot].at[pl.ds(r * k, k)], sem.at[slot])

    def start_tile(tile, slot):
        base = tile * tm

        def body(r, c):
            slab_copy(slot, r, tok_ref[base + r]).start()
            return c

        lax.fori_loop(0, tm, body, 0, unroll=_DMA_UNROLL)

    def wait_tile(slot):
        def body(r, c):
            slab_copy(slot, r, 0).wait()
            return c

        lax.fori_loop(0, tm, body, 0, unroll=_DMA_UNROLL)

    def consume(slot):
        for s in range(k):
            o_ref[:, s * _LANES:(s + 1) * _LANES] = _load_slab_col(bufs[slot], s, tm, k)

    _prefetched_tiles(t, pl.num_programs(0), ok_ref, start_tile, wait_tile, consume)

    @pl.when(ok_ref[t] == 0)
    def _():
        o_ref[...] = jnp.zeros_like(o_ref)


def _dispatch(h_slabs, plan, tm, cfg):
    k = cfg.d // 2 // _LANES
    rows = plan.row_tok.shape[0]
    return pl.pallas_call(
        functools.partial(_dispatch_kernel, tm=tm, k=k),
        out_shape=jax.ShapeDtypeStruct((rows, k * _LANES), _U32),
        grid_spec=pltpu.PrefetchScalarGridSpec(
            num_scalar_prefetch=2,
            grid=(rows // tm,),
            in_specs=[pl.BlockSpec(memory_space=pl.ANY)],
            out_specs=pl.BlockSpec((tm, k * _LANES), lambda t, tok, ok: (t, 0)),
            scratch_shapes=[pltpu.VMEM((tm * k, _LANES), _U32), pltpu.VMEM((tm * k, _LANES), _U32),
                            pltpu.SemaphoreType.DMA((2,))],
        ),
        compiler_params=_cparams("arbitrary"),
        name="moe_dispatch",
    )(plan.row_tok, plan.tile_ok, h_slabs)


def _gate_up_kernel(te_ref, new_ref, ok_ref, x_ref, wg_ref, wu_ref, o_ref, wgb_ref, wub_ref):
    t = pl.program_id(1)

    @pl.when(new_ref[t] == 1)
    def _():
        wgb_ref[...] = wg_ref[...].astype(_BF16)
        wub_ref[...] = wu_ref[...].astype(_BF16)

    @pl.when(ok_ref[t] == 1)
    def _():
        lo, hi = _unpack_bf16_pairs(x_ref[...])
        lo, hi = lo.astype(_BF16), hi.astype(_BF16)
        half = lo.shape[1]

        def proj(w_ref):
            return (jnp.dot(lo, w_ref[:half, :], preferred_element_type=_F32)
                    + jnp.dot(hi, w_ref[half:, :], preferred_element_type=_F32))

        a, b = proj(wgb_ref), proj(wub_ref)
        o_ref[...] = (a * jax.nn.sigmoid(a) * b).astype(o_ref.dtype)

    @pl.when(ok_ref[t] == 0)
    def _():
        o_ref[...] = jnp.zeros_like(o_ref)


def _gate_up(xs, w_gate, w_up, l, plan, tm):
    rows, dh = xs.shape
    d, de = w_gate.shape[2:]
    bn = _pick(de, (512, 256, 128))
    n_tiles = rows // tm
    w_spec = pl.BlockSpec((None, None, d, bn), lambda j, t, te, new, ok: (l, te[t], 0, j))
    return pl.pallas_call(
        _gate_up_kernel,
        out_shape=jax.ShapeDtypeStruct((rows, de), _BF16),
        grid_spec=pltpu.PrefetchScalarGridSpec(
            num_scalar_prefetch=3,
            grid=(de // bn, n_tiles),
            in_specs=[pl.BlockSpec((tm, dh), lambda j, t, te, new, ok: (t, 0)), w_spec, w_spec],
            out_specs=pl.BlockSpec((tm, bn), lambda j, t, te, new, ok: (t, j)),
            scratch_shapes=[pltpu.VMEM((d, bn), _BF16), pltpu.VMEM((d, bn), _BF16)],
        ),
        compiler_params=_cparams("arbitrary", "arbitrary"),
        name="moe_gate_up",
    )(plan.tile_e, plan.tile_new, plan.tile_ok, xs, w_gate, w_up)


def _down_kernel(te_ref, new_ref, ok_ref, x_ref, w_ref, o_ref, wb_ref):
    t = pl.program_id(1)

    @pl.when(new_ref[t] == 1)
    def _():
        wb_ref[...] = w_ref[...].astype(_BF16)

    @pl.when(ok_ref[t] == 1)
    def _():
        y = jnp.dot(x_ref[...], wb_ref[...], preferred_element_type=_F32)
        _store_slabs(o_ref, _pack_bf16_pairs(y))

    @pl.when(ok_ref[t] == 0)
    def _():
        o_ref[...] = jnp.zeros_like(o_ref)


def _down_block(d):
    return _pick(d, (2048, 1024, 512, 256))


def _down(hm, w_down, l, plan, tm):
    rows, de = hm.shape
    d = w_down.shape[3]
    bn = _down_block(d)
    k = bn // 2 // _LANES
    n_tiles = rows // tm
    return pl.pallas_call(
        _down_kernel,
        out_shape=jax.ShapeDtypeStruct((d // bn, rows * k, _LANES), _U32),
        grid_spec=pltpu.PrefetchScalarGridSpec(
            num_scalar_prefetch=3,
            grid=(d // bn, n_tiles),
            in_specs=[
                pl.BlockSpec((tm, de), lambda j, t, te, new, ok: (t, 0)),
                pl.BlockSpec((None, None, de, bn), lambda j, t, te, new, ok: (l, te[t], 0, j)),
            ],
            out_specs=pl.BlockSpec((None, tm * k, _LANES), lambda j, t, te, new, ok: (j, t, 0)),
            scratch_shapes=[pltpu.VMEM((de, bn), _BF16)],
        ),
        compiler_params=_cparams("arbitrary", "arbitrary"),
        name="moe_down",
    )(plan.tile_e, plan.tile_new, plan.tile_ok, hm, w_down)


def _combine_kernel(pos_ref, ok_ref, y_hbm, x_ref, g_ref, w_ref, o_ref, buf0, buf1, sem, *, bt, n, bn):
    t = pl.program_id(0)
    bufs = (buf0, buf1)
    nj = y_hbm.shape[0]
    k = bn // 2 // _LANES

    def slab_copy(slot, c, j, r, p):
        return pltpu.make_async_copy(y_hbm.at[j, pl.ds(p * k, k)], bufs[slot].at[c, j, pl.ds(r * k, k)],
                                     sem.at[slot])

    def start_tile(tile, slot):
        base = tile * bt

        def body(r, carry):
            for c in range(2):
                p = pos_ref[c * n + base + r]
                for j in range(nj):
                    slab_copy(slot, c, j, r, p).start()
            return carry

        lax.fori_loop(0, bt, body, 0, unroll=_DMA_UNROLL // 2)

    def wait_tile(slot):
        def body(r, carry):
            for c in range(2):
                for j in range(nj):
                    slab_copy(slot, c, j, r, 0).wait()
            return carry

        lax.fori_loop(0, bt, body, 0, unroll=_DMA_UNROLL // 2)

    def consume(slot):
        w0, w1 = w_ref[:, 0:1], w_ref[:, 1:2]
        g = g_ref[0]
        for j in range(nj):
            for s in range(k):
                lo0, hi0 = _unpack_bf16_pairs(_load_slab_col(bufs[slot].at[0, j], s, bt, k))
                lo1, hi1 = _unpack_bf16_pairs(_load_slab_col(bufs[slot].at[1, j], s, bt, k))
                for c0, y in ((j * bn + s * _LANES, w0 * lo0 + w1 * lo1),
                              (j * bn + bn // 2 + s * _LANES, w0 * hi0 + w1 * hi1)):
                    o_ref[:, c0:c0 + _LANES] = x_ref[:, c0:c0 + _LANES] + g[:, c0:c0 + _LANES] * y

    _prefetched_tiles(t, pl.num_programs(0), ok_ref, start_tile, wait_tile, consume)


def _combine(y, pos, wts_t, x, mod, l, cfg):
    n, d = x.shape
    bt = _row_tile(cfg, (256, 128))
    bn = _down_block(d)
    nj, _, lanes = y.shape
    n_t = n // bt
    buf = pltpu.VMEM((2, nj, bt * (bn // 2 // lanes), lanes), _U32)
    return pl.pallas_call(
        functools.partial(_combine_kernel, bt=bt, n=n, bn=bn),
        out_shape=jax.ShapeDtypeStruct((n, d), _F32),
        grid_spec=pltpu.PrefetchScalarGridSpec(
            num_scalar_prefetch=2,
            grid=(n_t,),
            in_specs=[
                pl.BlockSpec(memory_space=pl.ANY),
                pl.BlockSpec((bt, d), lambda i, pos, ok: (i, 0)),
                _mod_spec(l, _GATE2, d, lambda i, pos, ok: _group_of_tile(i, bt, cfg)),
                pl.BlockSpec((bt, 2), lambda i, pos, ok: (i, 0)),
            ],
            out_specs=pl.BlockSpec((bt, d), lambda i, pos, ok: (i, 0)),
            scratch_shapes=[buf, buf, pltpu.SemaphoreType.DMA((2,))],
        ),
        compiler_params=_cparams("arbitrary"),
        name="moe_combine",
    )(pos, jnp.ones((n_t,), jnp.int32), y, x, mod, wts_t)


def _rope_tables(cfg):
    t = cfg.s_s
    pos = jnp.arange(t, dtype=jnp.int32)
    row = (pos // _GRID_W).astype(_F32)
    col = (pos % _GRID_W).astype(_F32)

    def angles(rot_dim):
        quarter = rot_dim // 4
        inv_freq = _ROPE_THETA ** (-jnp.arange(quarter, dtype=_F32) / quarter)
        return jnp.concatenate([row[:, None] * inv_freq, col[:, None] * inv_freq], axis=-1)

    def stack(prompt_val, lat):
        lat = jnp.tile(lat, (cfg.b_s, 1))
        return jnp.concatenate([jnp.full((cfg.n_p, lat.shape[1]), prompt_val, _F32), lat], axis=0)

    ang_a, ang_b = angles(cfg.rope_a), angles(cfg.hd_b)
    cos_a, sin_a = jnp.cos(ang_a), jnp.sin(ang_a)
    cos_b, sin_b = jnp.cos(ang_b), jnp.sin(ang_b)
    return (stack(1.0, jnp.concatenate([cos_a, cos_a], -1)), stack(0.0, jnp.concatenate([sin_a, sin_a], -1)),
            stack(1.0, jnp.concatenate([cos_b, cos_b], -1)), stack(0.0, jnp.concatenate([-sin_b, sin_b], -1)))


def _keys_with_cache(own, cache, cfg):
    parts = []
    for b in range(cfg.b_s):
        parts += [cache[b].astype(own.dtype), own[cfg.n_p + b * cfg.s_s:cfg.n_p + (b + 1) * cfg.s_s]]
    parts.append(own[:cfg.n_p])
    return jnp.concatenate(parts, axis=0)


def _make_cfg(x_prompt, x_sample, cache_mla_ckv, cache_mla_krope, cache_gqa_k, w_ada, w_in, g_q_lat,
              w_uq, w_ukv, w_dw, w_router, w_gate):
    b, s, d = x_prompt.shape
    b_s, s_s, _ = x_sample.shape
    past, kv_lora = cache_mla_ckv.shape[2:]
    rope_a = cache_mla_krope.shape[3]
    kv_b, hd_b = cache_gqa_k.shape[3:]
    q_lora = g_q_lat.shape[1]
    conv_w, c_conv = w_dw.shape[1:]
    h_b = (w_in.shape[2] - q_lora - kv_lora - rope_a - 2 * kv_b * hd_b - 2 * c_conv) // hd_b
    hv = d - h_b * hd_b - c_conv
    h_a = (w_uq.shape[2] - w_ukv.shape[2] + hv) // rope_a
    v_a = hv // h_a
    nope_a = w_ukv.shape[2] // h_a - v_a
    return _Cfg(d=d, n_p=b * s, s_p=s, b_s=b_s, s_s=s_s, past=past, depth=w_ada.shape[0], q_lora=q_lora,
                kv_lora=kv_lora, rope_a=rope_a, nope_a=nope_a, v_a=v_a, h_a=h_a, h_b=h_b, kv_b=kv_b,
                hd_b=hd_b, c_conv=c_conv, conv_w=conv_w, n_exp=w_router.shape[1], d_exp=w_gate.shape[3])


def kernel(x_prompt, x_sample, c, c_ctx, cache_mla_ckv, cache_mla_krope, cache_gqa_k, cache_gqa_v, w_ada, b_ada, g_norm1, g_norm2, w_in, g_q_lat, g_kv_lat, w_uq, w_ukv, g_q_b, g_k_b, w_dw, b_dw, g_conv_ln, b_conv_ln, w_out, w_router, router_bias, w_gate, w_up, w_down, g_final):
    cfg = _make_cfg(x_prompt, x_sample, cache_mla_ckv, cache_mla_krope, cache_gqa_k, w_ada, w_in, g_q_lat,
                    w_uq, w_ukv, w_dw, w_router, w_gate)
    d, n_p, depth = cfg.d, cfg.n_p, cfg.depth
    batch, seq = x_prompt.shape[:2]
    kvw = cfg.kv_b * cfg.hd_b
    assert cfg.v_a == cfg.hd_b and 1 + cfg.b_s <= _MOD_ROWS

    x = jnp.concatenate([x_prompt.reshape(n_p, d), x_sample.reshape(cfg.n_s, d)], axis=0)

    m = jnp.concatenate([c_ctx[None, :], c, jnp.zeros((_MOD_ROWS - 1 - cfg.b_s, d), _F32)], axis=0)
    mod = _ada(jax.nn.silu(m).astype(_BF16), w_ada, b_ada).reshape(depth, _MOD_ROWS, 6, 1, d)

    vec = lambda a: a.reshape(depth, 1, a.shape[-1])
    g_norm1, g_norm2 = vec(g_norm1), vec(g_norm2)
    gains = [vec(g_q_lat), vec(g_kv_lat), vec(g_q_b), vec(g_k_b)]
    b_dw, g_conv_ln, b_conv_ln = vec(b_dw), vec(g_conv_ln), vec(b_conv_ln)
    w_in_r = _reorder_w_in(w_in, cfg)
    w_uq_r = _reorder_w_uq(w_uq, cfg)
    w_ukv_b = w_ukv.astype(_BF16)
    tabs = _rope_tables(cfg)
    tm = _pick(2 * cfg.n, (512, 256, 128))
    bq_s = _pick(cfg.s_s, (256, 128))

    ckvs, kpes, kbs, vbs = [], [], [], []
    for l in range(depth):
        h = _norm_mod(x, g_norm1, mod, l, cfg)
        p = _matmul(h, w_in_r, l, _F32, "in_proj")
        q_a, ckv, kpe, q_b, k_b, v_b, k_b16, v_b16, glu = _post_projection(p, w_uq_r, gains, l, tabs, cfg)
        ckvs.append(ckv[:n_p])
        kpes.append(kpe[:n_p])
        kbs.append(k_b[:n_p])
        vbs.append(v_b[:n_p])

        k_a, v_a = _kv_a(_keys_with_cache(ckv, cache_mla_ckv[:, l], cfg),
                         _keys_with_cache(kpe, cache_mla_krope[:, l], cfg), w_ukv_b, l, cfg)
        common_a = dict(group=1, dv=cfg.v_a, kv_head_major=True)
        a_p = _attention(q_a, k_a, v_a, n_seq=batch, tq=seq, tk=seq, q_row0=0, k_row0=cfg.b_s * cfg.tk_s,
                         hb=cfg.h_a, bq=seq, name="attn_a_prompt", **common_a)
        a_s = _attention(q_a, k_a, v_a, n_seq=cfg.b_s, tq=cfg.s_s, tk=cfg.tk_s, q_row0=n_p, k_row0=0,
                         hb=2 if cfg.h_a % 2 == 0 else 1, bq=bq_s, name="attn_a_latent", **common_a)
        k_bk = _keys_with_cache(k_b16, cache_gqa_k[:, l].reshape(cfg.b_s, cfg.past, kvw), cfg)
        v_cache = cache_gqa_v[:, l]
        ones = jnp.broadcast_to(_ones_block(1, cfg.hd_b, _F32), v_cache.shape)
        v_cache = jnp.concatenate([v_cache, ones], axis=-1).reshape(cfg.b_s, cfg.past, 2 * kvw)
        v_bk = _keys_with_cache(v_b16, v_cache, cfg)
        grp = cfg.h_b // cfg.kv_b
        common_b = dict(group=grp, dv=cfg.hd_b, kv_head_major=False)
        b_p = _attention(q_b, k_bk, v_bk, n_seq=batch, tq=seq, tk=seq, q_row0=0, k_row0=cfg.b_s * cfg.tk_s,
                         hb=cfg.h_b, bq=seq, name="attn_b_prompt", **common_b)
        b_s = _attention(q_b, k_bk, v_bk, n_seq=cfg.b_s, tq=cfg.s_s, tk=cfg.tk_s, q_row0=n_p, k_row0=0,
                         hb=grp, bq=bq_s, name="attn_b_latent", **common_b)
        out_c = _conv_module(glu, w_dw, b_dw, g_conv_ln, b_conv_ln, l, cfg)

        x = _out_proj(jnp.concatenate([a_p, a_s], axis=0), jnp.concatenate([b_p, b_s], axis=0), out_c,
                      w_out, x, mod, l, cfg)

        h2, idx, wts = _norm_router(x, g_norm2, mod, l, w_router, router_bias, cfg)
        plan, n_tiles = _route_plan(idx, cfg.n_exp, tm)
        xs = _dispatch(h2, plan, tm, cfg)
        hm = _gate_up(xs, w_gate, w_up, l, plan, tm)
        y = _down(hm, w_down, l, plan, tm)
        x = _combine(y, plan.pos, wts.T, x, mod, l, cfg)

    y_all = _final_norm(x, g_final)
    stack = lambda parts, tail: jnp.stack([a.reshape((batch, seq) + tail) for a in parts], axis=1)
    return (y_all[:n_p].reshape(batch, seq, d),
            y_all[n_p:].reshape(cfg.b_s, cfg.s_s, d),
            stack(ckvs, (cfg.kv_lora,)),
            stack(kpes, (cfg.rope_a,)),
            stack(kbs, (cfg.kv_b, cfg.hd_b)),
            stack(vbs, (cfg.kv_b, cfg.hd_b)))
```

```python
import functools
from typing import NamedTuple

import jax
import jax.numpy as jnp
from jax import lax
from jax.experimental import pallas as pl
from jax.experimental.pallas import tpu as pltpu

_F32 = jnp.float32
_BF16 = jnp.bfloat16
_U32 = jnp.uint32

_GRID_W = 64
_N_GROUPS = 4
_ROPE_THETA = 10000.0
_EPS = 1e-6
_LOG2E = 1.4426950408889634

_V7X_VMEM_BYTES = 64 * 1024 * 1024
_VMEM_LIMIT = _V7X_VMEM_BYTES - 8 * 1024 * 1024
_LANES = 128
_SUBLANES = 8
_MOD_ROWS = 16
_SHIFT1, _SCALE1, _GATE1, _SHIFT2, _SCALE2, _GATE2 = range(6)


class _Cfg(NamedTuple):
    d: int
    n_p: int
    s_p: int
    b_s: int
    s_s: int
    past: int
    depth: int
    q_lora: int
    kv_lora: int
    rope_a: int
    nope_a: int
    v_a: int
    h_a: int
    h_b: int
    kv_b: int
    hd_b: int
    c_conv: int
    conv_w: int
    n_exp: int
    d_exp: int

    @property
    def n_s(self):
        return self.b_s * self.s_s

    @property
    def n(self):
        return self.n_p + self.n_s

    @property
    def tk_s(self):
        return self.past + self.s_s


def _pick(n, candidates):
    for c in candidates:
        if n % c == 0:
            return c
    raise ValueError(f"no tile in {candidates} divides {n}")


def _row_tile(cfg, candidates):
    for c in candidates:
        if cfg.n_p % c == 0 and cfg.s_s % c == 0:
            return c
    raise ValueError(f"no row tile in {candidates}")


def _cparams(*sem):
    return pltpu.CompilerParams(dimension_semantics=sem, vmem_limit_bytes=_VMEM_LIMIT)


def _group_of_tile(i, bm, cfg):
    r0 = i * bm
    return jnp.where(r0 < cfg.n_p, 0, 1 + (r0 - cfg.n_p) // cfg.s_s)


def _mod_spec(l, which, width, group_fn, col_fn=None):
    def index(*ids):
        return (l, group_fn(*ids), which, 0, 0 if col_fn is None else col_fn(*ids))
    return pl.BlockSpec((None, 1, None, 1, width), index)


def _layer_spec(a, l):
    nd = a.ndim - 1
    return pl.BlockSpec((None,) + a.shape[1:], lambda *ids: (l,) + (0,) * nd)


def _pack_bf16_pairs(x):
    w = x.shape[1] // 2
    r = x.astype(_BF16).astype(_F32)
    lo = lax.bitcast_convert_type(r[:, :w], _U32) >> 16
    hi = lax.bitcast_convert_type(r[:, w:], _U32) & _U32(0xFFFF0000)
    return lo | hi


def _unpack_bf16_pairs(p):
    lo = lax.bitcast_convert_type(p << 16, _F32)
    hi = lax.bitcast_convert_type(p & _U32(0xFFFF0000), _F32)
    return lo, hi


def _store_slabs(o_ref, x):
    rows, w = x.shape
    k = w // _LANES
    for s in range(k):
        o_ref[pl.ds(s, rows, stride=k), :] = x[:, s * _LANES:(s + 1) * _LANES]


def _load_slab_col(ref, s, rows, k):
    return ref[pl.ds(s, rows, stride=k), :]


def _ada_kernel(s_ref, w_ref, b_ref, o_ref):
    w = w_ref[0].astype(_BF16)
    o_ref[0] = jnp.dot(s_ref[...], w, preferred_element_type=_F32) + b_ref[0]


def _ada(sm, w_ada, b_ada):
    depth, d, n6 = w_ada.shape
    bn = _pick(n6, (512, 256, 128))
    return pl.pallas_call(
        _ada_kernel,
        out_shape=jax.ShapeDtypeStruct((depth, _MOD_ROWS, n6), _F32),
        grid=(depth, n6 // bn),
        in_specs=[
            pl.BlockSpec((_MOD_ROWS, d), lambda l, j: (0, 0)),
            pl.BlockSpec((1, d, bn), lambda l, j: (l, 0, j)),
            pl.BlockSpec((1, 1, bn), lambda l, j: (l, 0, j)),
        ],
        out_specs=pl.BlockSpec((1, _MOD_ROWS, bn), lambda l, j: (l, 0, j)),
        compiler_params=_cparams("arbitrary", "arbitrary"),
        name="ada",
    )(sm, w_ada, b_ada.reshape(depth, 1, n6))


def _rms(x, g):
    return x * lax.rsqrt(jnp.mean(x * x, axis=-1, keepdims=True) + _EPS) * g


def _norm_mod_kernel(x_ref, g_ref, sc_ref, sh_ref, o_ref):
    h = _rms(x_ref[...], g_ref[...]) * (1.0 + sc_ref[0]) + sh_ref[0]
    o_ref[...] = h.astype(o_ref.dtype)


def _norm_mod(x, g, mod, l, cfg):
    n, d = x.shape
    bm = _row_tile(cfg, (256, 128))
    grp = lambda i: _group_of_tile(i, bm, cfg)
    return pl.pallas_call(
        _norm_mod_kernel,
        out_shape=jax.ShapeDtypeStruct((n, d), _BF16),
        grid=(n // bm,),
        in_specs=[
            pl.BlockSpec((bm, d), lambda i: (i, 0)),
            _layer_spec(g, l),
            _mod_spec(l, _SCALE1, d, grp),
            _mod_spec(l, _SHIFT1, d, grp),
        ],
        out_specs=pl.BlockSpec((bm, d), lambda i: (i, 0)),
        compiler_params=_cparams("arbitrary"),
        name="norm_mod",
    )(x, g, mod, mod)


def _mm_kernel(x_ref, w_ref, o_ref):
    o_ref[...] = jnp.dot(x_ref[...], w_ref[...], preferred_element_type=_F32).astype(o_ref.dtype)


def _matmul(x, w, l, out_dtype, name):
    m, k = x.shape
    n = w.shape[2]
    bm = _pick(m, (1024, 512, 256, 128))
    bn = _pick(n, (512, 256, 128))
    return pl.pallas_call(
        _mm_kernel,
        out_shape=jax.ShapeDtypeStruct((m, n), out_dtype),
        grid=(n // bn, m // bm),
        in_specs=[pl.BlockSpec((bm, k), lambda j, i: (i, 0)),
                  pl.BlockSpec((None, k, bn), lambda j, i: (l, 0, j))],
        out_specs=pl.BlockSpec((bm, bn), lambda j, i: (i, j)),
        compiler_params=_cparams("arbitrary", "arbitrary"),
        name=name,
    )(x, w)


class _InCols(NamedTuple):
    q_lat: int
    kv_lat: int
    q_b: int
    k_b: int
    v_b: int
    u_c: int
    k_pe: int
    k_pe_rot: int
    width: int


def _in_cols(cfg):
    offs, acc = [], 0
    for w in (cfg.q_lora, cfg.kv_lora, cfg.h_b * cfg.hd_b, cfg.kv_b * cfg.hd_b, cfg.kv_b * cfg.hd_b,
              2 * cfg.c_conv, cfg.rope_a, cfg.rope_a):
        offs.append(acc)
        acc += w
    width = -(-acc // 512) * 512
    return _InCols(*offs, width)


def _rotate_half_cols(w):
    half = w.shape[-1] // 2
    return jnp.concatenate([-w[..., half:], w[..., :half]], axis=-1)


def _reorder_w_in_kernel(w_ref, o_ref, *, o, r):
    w = w_ref[...]
    src_w = w.shape[1]
    k_pe = w[:, o:o + r]
    o_ref[:, :o] = w[:, :o].astype(o_ref.dtype)
    o_ref[:, o:src_w - r] = w[:, o + r:].astype(o_ref.dtype)
    tail = [k_pe, _rotate_half_cols(k_pe)]
    pad = o_ref.shape[1] - src_w - r
    if pad:
        tail.append(jnp.zeros((w.shape[0], pad), w.dtype))
    o_ref[:, src_w - r:] = jnp.concatenate(tail, axis=-1).astype(o_ref.dtype)


def _reorder_w_in(w_in, cfg):
    depth, k, src_w = w_in.shape
    cols = _in_cols(cfg)
    bk = _pick(k, (256, 128))
    return pl.pallas_call(
        functools.partial(_reorder_w_in_kernel, o=cfg.q_lora + cfg.kv_lora, r=cfg.rope_a),
        out_shape=jax.ShapeDtypeStruct((depth, k, cols.width), _BF16),
        grid=(depth, k // bk),
        in_specs=[pl.BlockSpec((None, bk, src_w), lambda l, i: (l, i, 0))],
        out_specs=pl.BlockSpec((None, bk, cols.width), lambda l, i: (l, i, 0)),
        compiler_params=_cparams("arbitrary", "arbitrary"),
        name="reorder_w_in",
    )(w_in)


def _reorder_w_uq(w_uq, cfg):
    depth, ql, _ = w_uq.shape
    w = w_uq.reshape(depth, ql, cfg.h_a, cfg.nope_a + cfg.rope_a)
    nope = w[..., :cfg.nope_a].reshape(depth, ql, -1)
    pe = w[..., cfg.nope_a:]
    rot = _rotate_half_cols(pe).reshape(depth, ql, -1)
    return jnp.concatenate([nope, pe.reshape(depth, ql, -1), rot], axis=-1).astype(_BF16)


def _post_kernel(p_ref, wuq_ref, gq_ref, gkv_ref, gqb_ref, gkb_ref, ca_ref, sa_ref, cb_ref, sb_ref,
                 qa_ref, ckv_ref, kpe_ref, qb_ref, kb_ref, vb_ref, kb16_ref, vb16_ref, glu_ref, *, cfg, cols):
    ra, hd = cfg.rope_a, cfg.hd_b
    ca, sa, cb, sb = ca_ref[...], sa_ref[...], cb_ref[...], sb_ref[...]

    qn = _rms(p_ref[:, cols.q_lat:cols.q_lat + cfg.q_lora], gq_ref[...]).astype(_BF16)
    qa = jnp.dot(qn, wuq_ref[...], preferred_element_type=_F32)
    pe0 = cfg.h_a * cfg.nope_a
    rot0 = pe0 + cfg.h_a * ra
    scale_a = _LOG2E * (cfg.nope_a + ra) ** -0.5
    for h in range(cfg.h_a):
        nope = qa[:, h * cfg.nope_a:(h + 1) * cfg.nope_a]
        pe = qa[:, pe0 + h * ra:pe0 + (h + 1) * ra] * ca + qa[:, rot0 + h * ra:rot0 + (h + 1) * ra] * sa
        qa_ref[h] = (jnp.concatenate([nope, pe], axis=-1) * scale_a).astype(qa_ref.dtype)

    ckv_ref[...] = _rms(p_ref[:, cols.kv_lat:cols.kv_lat + cfg.kv_lora], gkv_ref[...])
    kpe_ref[...] = p_ref[:, cols.k_pe:cols.k_pe + ra] * ca + p_ref[:, cols.k_pe_rot:cols.k_pe_rot + ra] * sa

    def head_norm_rope(x, g):
        y = _rms(x, g)
        return y * cb + pltpu.roll(y, hd // 2, axis=1) * sb

    scale_b = _LOG2E * hd ** -0.5
    for h in range(cfg.h_b):
        x = p_ref[:, cols.q_b + h * hd:cols.q_b + (h + 1) * hd]
        qb_ref[h] = (head_norm_rope(x, gqb_ref[...]) * scale_b).astype(qb_ref.dtype)
    for h in range(cfg.kv_b):
        x = p_ref[:, cols.k_b + h * hd:cols.k_b + (h + 1) * hd]
        kh = head_norm_rope(x, gkb_ref[...])
        kb_ref[:, h * hd:(h + 1) * hd] = kh
        kb16_ref[:, h * hd:(h + 1) * hd] = kh.astype(kb16_ref.dtype)
    vb = p_ref[:, cols.v_b:cols.v_b + cfg.kv_b * hd]
    vb_ref[...] = vb
    vb16_ref[...] = vb.astype(vb16_ref.dtype)

    a = p_ref[:, cols.u_c:cols.u_c + cfg.c_conv]
    gate = p_ref[:, cols.u_c + cfg.c_conv:cols.u_c + 2 * cfg.c_conv]
    glu_ref[...] = a * jax.nn.sigmoid(gate)


def _post_projection(p, w_uq_r, gains, l, tabs, cfg):
    n = cfg.n
    cols = _in_cols(cfg)
    bm = _pick(n, (256, 128))
    dk_a = cfg.nope_a + cfg.rope_a
    kvw = cfg.kv_b * cfg.hd_b
    row = lambda w: pl.BlockSpec((bm, w), lambda i: (i, 0))
    out_shape = (
        jax.ShapeDtypeStruct((cfg.h_a, n, dk_a), _BF16),
        jax.ShapeDtypeStruct((n, cfg.kv_lora), _F32),
        jax.ShapeDtypeStruct((n, cfg.rope_a), _F32),
        jax.ShapeDtypeStruct((cfg.h_b, n, cfg.hd_b), _BF16),
        jax.ShapeDtypeStruct((n, kvw), _F32),
        jax.ShapeDtypeStruct((n, kvw), _F32),
        jax.ShapeDtypeStruct((n, kvw), _BF16),
        jax.ShapeDtypeStruct((n, kvw), _BF16),
        jax.ShapeDtypeStruct((n, cfg.c_conv), _F32),
    )
    out_specs = (
        pl.BlockSpec((cfg.h_a, bm, dk_a), lambda i: (0, i, 0)),
        row(cfg.kv_lora),
        row(cfg.rope_a),
        pl.BlockSpec((cfg.h_b, bm, cfg.hd_b), lambda i: (0, i, 0)),
        row(kvw),
        row(kvw),
        row(kvw),
        row(kvw),
        row(cfg.c_conv),
    )
    return pl.pallas_call(
        functools.partial(_post_kernel, cfg=cfg, cols=cols),
        out_shape=out_shape,
        grid=(n // bm,),
        in_specs=[row(cols.width), _layer_spec(w_uq_r, l)] + [_layer_spec(g, l) for g in gains]
        + [row(cfg.rope_a), row(cfg.rope_a), row(cfg.hd_b), row(cfg.hd_b)],
        out_specs=out_specs,
        compiler_params=_cparams("arbitrary"),
        name="post_projection",
    )(p, w_uq_r, *gains, *tabs)


def _kv_a_kernel(ckv_ref, kpe_ref, w_ref, k_ref, v_ref, *, cfg):
    kv = jnp.dot(ckv_ref[...].astype(_BF16), w_ref[...], preferred_element_type=_F32)
    kpe = kpe_ref[...]
    hw = cfg.nope_a + cfg.v_a
    for h in range(cfg.h_a):
        k_ref[h] = jnp.concatenate([kv[:, h * hw:h * hw + cfg.nope_a], kpe], axis=-1).astype(k_ref.dtype)
        v_ref[h] = kv[:, h * hw + cfg.nope_a:(h + 1) * hw].astype(v_ref.dtype)


def _kv_a(ckv_keys, kpe_keys, w_ukv, l, cfg):
    t = ckv_keys.shape[0]
    bm = _pick(t, (256, 128))
    dk = cfg.nope_a + cfg.rope_a
    return pl.pallas_call(
        functools.partial(_kv_a_kernel, cfg=cfg),
        out_shape=(jax.ShapeDtypeStruct((cfg.h_a, t, dk), _BF16),
                   jax.ShapeDtypeStruct((cfg.h_a, t, cfg.v_a), _BF16)),
        grid=(t // bm,),
        in_specs=[
            pl.BlockSpec((bm, cfg.kv_lora), lambda i: (i, 0)),
            pl.BlockSpec((bm, cfg.rope_a), lambda i: (i, 0)),
            _layer_spec(w_ukv, l),
        ],
        out_specs=(pl.BlockSpec((cfg.h_a, bm, dk), lambda i: (0, i, 0)),
                   pl.BlockSpec((cfg.h_a, bm, cfg.v_a), lambda i: (0, i, 0))),
        compiler_params=_cparams("arbitrary"),
        name="kv_a",
    )(ckv_keys, kpe_keys, w_ukv)


def _attn_kernel(q_ref, k_ref, v_ref, o_ref, *, hb, group, dv, kv_head_major):
    for j in range(hb):
        kvj = j // group
        q = q_ref[j]
        if kv_head_major:
            k, v = k_ref[kvj], v_ref[kvj]
        else:
            k, v = k_ref[:, kvj * dv:(kvj + 1) * dv], v_ref[:, kvj * dv:(kvj + 1) * dv]
        s = lax.dot_general(q, k, (((1,), (1,)), ((), ())), preferred_element_type=_F32)
        m = jnp.max(s, axis=-1, keepdims=True)
        p = jnp.exp2(s - m)
        l = jnp.sum(p, axis=-1, keepdims=True)
        o = jnp.dot(p.astype(_BF16), v, preferred_element_type=_F32) / l
        o_ref[:, j * dv:(j + 1) * dv] = o.astype(o_ref.dtype)


def _attention(q, k, v, *, n_seq, tq, tk, q_row0, k_row0, hb, group, dv, bq, kv_head_major, name):
    hq, _, dk = q.shape
    hkb = hb // group
    nq = tq // bq
    qb0, kb0 = q_row0 // bq, k_row0 // tk
    assert q_row0 % bq == 0 and k_row0 % tk == 0 and hq % hb == 0 and hb % group == 0
    if kv_head_major:
        k_spec = pl.BlockSpec((hkb, tk, k.shape[2]), lambda s, h, i: (h, kb0 + s, 0))
        v_spec = pl.BlockSpec((hkb, tk, v.shape[2]), lambda s, h, i: (h, kb0 + s, 0))
    else:
        k_spec = pl.BlockSpec((tk, hkb * dv), lambda s, h, i: (kb0 + s, h))
        v_spec = pl.BlockSpec((tk, hkb * dv), lambda s, h, i: (kb0 + s, h))
    return pl.pallas_call(
        functools.partial(_attn_kernel, hb=hb, group=group, dv=dv, kv_head_major=kv_head_major),
        out_shape=jax.ShapeDtypeStruct((n_seq * tq, hq * dv), _BF16),
        grid=(n_seq, hq // hb, nq),
        in_specs=[pl.BlockSpec((hb, bq, dk), lambda s, h, i: (h, qb0 + s * nq + i, 0)), k_spec, v_spec],
        out_specs=pl.BlockSpec((bq, hb * dv), lambda s, h, i: (s * nq + i, h)),
        compiler_params=_cparams("arbitrary", "arbitrary", "arbitrary"),
        name=name,
    )(q, k, v)


_HALO = 16
_CONV_ROWS = 32


def _conv_kernel(prev_ref, cur_ref, next_ref, w_ref, bdw_ref, g_ref, b_ref, o_ref, buf_ref, sh_ref, *, cfg, bt):
    i = pl.program_id(0)
    n_pt = cfg.n_p // bt
    per_seq = cfg.s_s // bt
    j = jnp.maximum(i - n_pt, 0) % per_seq
    lat = i >= n_pt
    p_per = cfg.s_p // bt
    jp = i % p_per
    has_prev = jnp.where(lat, j > 0, jp > 0)
    has_next = jnp.where(lat, j < per_seq - 1, jp < p_per - 1)
    buf_ref[0:_HALO, :] = jnp.where(has_prev, prev_ref[...], 0.0)
    buf_ref[_HALO:_HALO + bt, :] = cur_ref[...]
    buf_ref[_HALO + bt:2 * _HALO + bt, :] = jnp.where(has_next, next_ref[...], 0.0)
    span = sh_ref.shape[1]
    for s in range(_SUBLANES):
        sh_ref[s] = buf_ref[s:s + span, :]
    half = cfg.conv_w // 2
    for r in range(0, bt, _CONV_ROWS):
        acc = jnp.zeros((_CONV_ROWS, cfg.c_conv), _F32)
        for t in range(cfg.conv_w):
            lo = _HALO + r + t - half
            s = lo % _SUBLANES
            acc = acc + sh_ref[s, lo - s:lo - s + _CONV_ROWS, :] * w_ref[t:t + 1, :]
        y = acc + bdw_ref[...]
        mu = jnp.mean(y, axis=-1, keepdims=True)
        yc = y - mu
        var = jnp.mean(yc * yc, axis=-1, keepdims=True)
        z = yc * lax.rsqrt(var + _EPS) * g_ref[...] + b_ref[...]
        o_ref[r:r + _CONV_ROWS, :] = (z * jax.nn.sigmoid(z)).astype(o_ref.dtype)


def _conv_module(v, w_dw, b_dw, g_ln, b_ln, l, cfg):
    n, c = v.shape
    bt = _pick(cfg.s_p, (256, 128))
    assert cfg.s_s % bt == 0 and cfg.conv_w // 2 < _HALO and bt % _CONV_ROWS == 0
    hb = bt // _HALO
    last = n // _HALO - 1
    span = bt + 2 * _HALO - _SUBLANES
    return pl.pallas_call(
        functools.partial(_conv_kernel, cfg=cfg, bt=bt),
        out_shape=jax.ShapeDtypeStruct((n, c), _BF16),
        grid=(n // bt,),
        in_specs=[
            pl.BlockSpec((_HALO, c), lambda i: (jnp.maximum(i * hb - 1, 0), 0)),
            pl.BlockSpec((bt, c), lambda i: (i, 0)),
            pl.BlockSpec((_HALO, c), lambda i: (jnp.minimum((i + 1) * hb, last), 0)),
            _layer_spec(w_dw, l), _layer_spec(b_dw, l), _layer_spec(g_ln, l), _layer_spec(b_ln, l),
        ],
        out_specs=pl.BlockSpec((bt, c), lambda i: (i, 0)),
        scratch_shapes=[pltpu.VMEM((bt + 2 * _HALO, c), _F32), pltpu.VMEM((_SUBLANES, span, c), _F32)],
        compiler_params=_cparams("arbitrary"),
        name="conv_module",
    )(v, v, v, w_dw, b_dw, g_ln, b_ln)


def _out_proj_kernel(a_ref, b_ref, c_ref, w_ref, x_ref, g_ref, o_ref, wb_ref, *, wa, wbw):
    @pl.when(pl.program_id(1) == 0)
    def _():
        wb_ref[...] = w_ref[...].astype(_BF16)

    acc = jnp.dot(a_ref[...], wb_ref[0:wa, :], preferred_element_type=_F32)
    acc = acc + jnp.dot(b_ref[...], wb_ref[wa:wa + wbw, :], preferred_element_type=_F32)
    acc = acc + jnp.dot(c_ref[...], wb_ref[wa + wbw:, :], preferred_element_type=_F32)
    o_ref[...] = x_ref[...] + g_ref[0] * acc


def _out_proj(out_a, out_b, out_c, w_out, x, mod, l, cfg):
    n, d = x.shape
    wa, wbw, wc = out_a.shape[1], out_b.shape[1], out_c.shape[1]
    bm = _row_tile(cfg, (1024, 512, 256, 128))
    bn = _pick(d, (512, 256, 128))
    return pl.pallas_call(
        functools.partial(_out_proj_kernel, wa=wa, wbw=wbw),
        out_shape=jax.ShapeDtypeStruct((n, d), _F32),
        grid=(d // bn, n // bm),
        in_specs=[
            pl.BlockSpec((bm, wa), lambda j, i: (i, 0)),
            pl.BlockSpec((bm, wbw), lambda j, i: (i, 0)),
            pl.BlockSpec((bm, wc), lambda j, i: (i, 0)),
            pl.BlockSpec((None, d, bn), lambda j, i: (l, 0, j)),
            pl.BlockSpec((bm, bn), lambda j, i: (i, j)),
            _mod_spec(l, _GATE1, bn, lambda j, i: _group_of_tile(i, bm, cfg), lambda j, i: j),
        ],
        out_specs=pl.BlockSpec((bm, bn), lambda j, i: (i, j)),
        scratch_shapes=[pltpu.VMEM((d, bn), _BF16)],
        compiler_params=_cparams("arbitrary", "arbitrary"),
        name="out_proj",
    )(out_a, out_b, out_c, w_out, x, mod)


def _norm_router_kernel(x_ref, g_ref, sc_ref, sh_ref, wr_ref, br_ref, h_ref, idx_ref, wts_ref, *, n_exp):
    h = _rms(x_ref[...], g_ref[...]) * (1.0 + sc_ref[0]) + sh_ref[0]
    _store_slabs(h_ref, _pack_bf16_pairs(h))
    logits = jnp.dot(h.astype(_BF16), wr_ref[...], preferred_element_type=_F32)
    scores = jax.nn.sigmoid(logits)
    biased = scores + br_ref[...]
    sc_t = scores.T
    bi_t = biased.T
    epg = n_exp // _N_GROUPS
    row = lambda a, e: a[e:e + 1, :]

    group_scores = []
    for g in range(_N_GROUPS):
        vals = [row(bi_t, g * epg + i) for i in range(epg)]
        best = None
        for i in range(epg):
            for j in range(i + 1, epg):
                pair = vals[i] + vals[j]
                best = pair if best is None else jnp.maximum(best, pair)
        group_scores.append(best)
    sel = jnp.zeros_like(group_scores[0], dtype=jnp.int32)
    best = group_scores[0]
    for g in range(1, _N_GROUPS):
        upd = group_scores[g] > best
        sel = jnp.where(upd, g, sel)
        best = jnp.where(upd, group_scores[g], best)

    def in_group(a, i):
        out = row(a, i)
        for g in range(1, _N_GROUPS):
            out = jnp.where(sel == g, row(a, g * epg + i), out)
        return out

    b = [in_group(bi_t, i) for i in range(epg)]
    u = [in_group(sc_t, i) for i in range(epg)]
    i1, v1, w1 = jnp.zeros_like(sel), b[0], u[0]
    for i in range(1, epg):
        upd = b[i] > v1
        i1, v1, w1 = jnp.where(upd, i, i1), jnp.where(upd, b[i], v1), jnp.where(upd, u[i], w1)
    i2 = jnp.zeros_like(sel)
    v2 = jnp.full_like(v1, -jnp.inf)
    w2 = jnp.zeros_like(w1)
    for i in range(epg):
        upd = jnp.logical_and(i1 != i, b[i] > v2)
        i2, v2, w2 = jnp.where(upd, i, i2), jnp.where(upd, b[i], v2), jnp.where(upd, u[i], w2)
    den = w1 + w2
    idx_ref[0:1, :] = sel * epg + i1
    idx_ref[1:2, :] = sel * epg + i2
    wts_ref[0:1, :] = w1 / den
    wts_ref[1:2, :] = w2 / den


def _norm_router(x, g, mod, l, w_router, router_bias, cfg):
    n, d = x.shape
    bm = _row_tile(cfg, (256, 128))
    e = cfg.n_exp
    k_slab = d // 2 // _LANES
    wr = jnp.zeros((d, _LANES), _BF16).at[:, :e].set(w_router.astype(_BF16))
    br = jnp.zeros((1, _LANES), _F32).at[0, :e].set(router_bias)
    grp = lambda i: _group_of_tile(i, bm, cfg)
    return pl.pallas_call(
        functools.partial(_norm_router_kernel, n_exp=e),
        out_shape=(jax.ShapeDtypeStruct((n * k_slab, _LANES), _U32),
                   jax.ShapeDtypeStruct((2, n), jnp.int32),
                   jax.ShapeDtypeStruct((2, n), _F32)),
        grid=(n // bm,),
        in_specs=[
            pl.BlockSpec((bm, d), lambda i: (i, 0)),
            _layer_spec(g, l),
            _mod_spec(l, _SCALE2, d, grp),
            _mod_spec(l, _SHIFT2, d, grp),
            pl.BlockSpec((d, _LANES), lambda i: (0, 0)),
            pl.BlockSpec((1, _LANES), lambda i: (0, 0)),
        ],
        out_specs=(pl.BlockSpec((bm * k_slab, _LANES), lambda i: (i, 0)),
                   pl.BlockSpec((2, bm), lambda i: (0, i)),
                   pl.BlockSpec((2, bm), lambda i: (0, i))),
        compiler_params=_cparams("arbitrary"),
        name="norm_router",
    )(x, g, mod, mod, wr, br)


class _Plan(NamedTuple):
    pos: jax.Array
    row_tok: jax.Array
    tile_e: jax.Array
    tile_new: jax.Array
    tile_ok: jax.Array


def _route_plan(idx, n_exp, tm):
    n = idx.shape[1]
    e_flat = idx.reshape(-1)
    onehot = (e_flat[:, None] == jnp.arange(n_exp, dtype=jnp.int32)[None, :]).astype(jnp.int32)
    rank = jnp.sum((jnp.cumsum(onehot, axis=0) - onehot) * onehot, axis=1)
    counts = jnp.sum(onehot, axis=0)
    tiles_per = (counts + tm - 1) // tm
    tile_end = jnp.cumsum(tiles_per)
    tile_start = tile_end - tiles_per
    pos = (tile_start[e_flat] * tm + rank).astype(jnp.int32)
    n_tiles = (2 * n) // tm + n_exp
    tok = jnp.arange(2 * n, dtype=jnp.int32) % n
    row_tok = jnp.zeros((n_tiles * tm,), jnp.int32).at[pos].set(tok)
    tid = jnp.arange(n_tiles, dtype=jnp.int32)
    total = tile_end[-1]
    ok = (tid < total).astype(jnp.int32)
    te = jnp.searchsorted(tile_end, jnp.minimum(tid, total - 1), side="right").astype(jnp.int32)
    new = jnp.concatenate([jnp.ones((1,), jnp.int32), (te[1:] != te[:-1]).astype(jnp.int32)])
    return _Plan(pos, row_tok, te, new, ok), n_tiles


_DMA_UNROLL = 8
_DMA_QUEUES = 2


def _prefetched_tiles(t, n_t, ok_ref, start_tile, wait_tile, consume):
    slot = t % 2
    nxt = jnp.minimum(t + 1, n_t - 1)
    fetch_next = jnp.logical_and(t + 1 < n_t, ok_ref[nxt] == 1)

    @pl.when(jnp.logical_and(t == 0, ok_ref[0] == 1))
    def _():
        start_tile(0, 0)

    for sl in (0, 1):
        @pl.when(jnp.logical_and(fetch_next, slot == 1 - sl))
        def _():
            start_tile(t + 1, sl)

    for sl in (0, 1):
        @pl.when(jnp.logical_and(ok_ref[t] == 1, slot == sl))
        def _():
            wait_tile(sl)
            consume(sl)


def _dispatch_kernel(tok_ref, ok_ref, h_hbm, o_ref, buf0, buf1, sem, *, tm, k):
    t = pl.program_id(0)
    bufs = (buf0, buf1)

    def slab_copy(slot, r, tok):
        return pltpu.make_async_copy(h_hbm.at[pl.ds(tok * k, k)], bufs[slot].at[pl.ds(r * k, k)], sem.at[slot])

    def start_tile(tile, slot):
        base = tile * tm

        def body(r2, c):
            for q in range(_DMA_QUEUES):
                r = _DMA_QUEUES * r2 + q
                slab_copy(slot, r, tok_ref[base + r]).start(priority=q)
            return c

        lax.fori_loop(0, tm // _DMA_QUEUES, body, 0, unroll=_DMA_UNROLL // _DMA_QUEUES)

    def wait_tile(slot):
        def body(r, c):
            slab_copy(slot, r, 0).wait()
            return c

        lax.fori_loop(0, tm, body, 0, unroll=_DMA_UNROLL)

    def consume(slot):
        for s in range(k):
            o_ref[:, s * _LANES:(s + 1) * _LANES] = _load_slab_col(bufs[slot], s, tm, k)

    _prefetched_tiles(t, pl.num_programs(0), ok_ref, start_tile, wait_tile, consume)

    @pl.when(ok_ref[t] == 0)
    def _():
        o_ref[...] = jnp.zeros_like(o_ref)


def _dispatch(h_slabs, plan, tm, cfg):
    k = cfg.d // 2 // _LANES
    rows = plan.row_tok.shape[0]
    return pl.pallas_call(
        functools.partial(_dispatch_kernel, tm=tm, k=k),
        out_shape=jax.ShapeDtypeStruct((rows, k * _LANES), _U32),
        grid_spec=pltpu.PrefetchScalarGridSpec(
            num_scalar_prefetch=2,
            grid=(rows // tm,),
            in_specs=[pl.BlockSpec(memory_space=pl.ANY)],
            out_specs=pl.BlockSpec((tm, k * _LANES), lambda t, tok, ok: (t, 0)),
            scratch_shapes=[pltpu.VMEM((tm * k, _LANES), _U32), pltpu.VMEM((tm * k, _LANES), _U32),
                            pltpu.SemaphoreType.DMA((2,))],
        ),
        compiler_params=_cparams("arbitrary"),
        name="moe_dispatch",
    )(plan.row_tok, plan.tile_ok, h_slabs)


def _gate_up_kernel(te_ref, new_ref, ok_ref, x_ref, wg_ref, wu_ref, o_ref, wgb_ref, wub_ref):
    t = pl.program_id(1)

    @pl.when(new_ref[t] == 1)
    def _():
        wgb_ref[...] = wg_ref[...].astype(_BF16)
        wub_ref[...] = wu_ref[...].astype(_BF16)

    @pl.when(ok_ref[t] == 1)
    def _():
        lo, hi = _unpack_bf16_pairs(x_ref[...])
        lo, hi = lo.astype(_BF16), hi.astype(_BF16)
        half = lo.shape[1]

        def proj(w_ref):
            return (jnp.dot(lo, w_ref[:half, :], preferred_element_type=_F32)
                    + jnp.dot(hi, w_ref[half:, :], preferred_element_type=_F32))

        a, b = proj(wgb_ref), proj(wub_ref)
        o_ref[...] = (a * jax.nn.sigmoid(a) * b).astype(o_ref.dtype)

    @pl.when(ok_ref[t] == 0)
    def _():
        o_ref[...] = jnp.zeros_like(o_ref)


def _gate_up(xs, w_gate, w_up, l, plan, tm):
    rows, dh = xs.shape
    d, de = w_gate.shape[2:]
    bn = _pick(de, (512, 256, 128))
    n_tiles = rows // tm
    w_spec = pl.BlockSpec((None, None, d, bn), lambda j, t, te, new, ok: (l, te[t], 0, j))
    return pl.pallas_call(
        _gate_up_kernel,
        out_shape=jax.ShapeDtypeStruct((rows, de), _BF16),
        grid_spec=pltpu.PrefetchScalarGridSpec(
            num_scalar_prefetch=3,
            grid=(de // bn, n_tiles),
            in_specs=[pl.BlockSpec((tm, dh), lambda j, t, te, new, ok: (t, 0)), w_spec, w_spec],
            out_specs=pl.BlockSpec((tm, bn), lambda j, t, te, new, ok: (t, j)),
            scratch_shapes=[pltpu.VMEM((d, bn), _BF16), pltpu.VMEM((d, bn), _BF16)],
        ),
        compiler_params=_cparams("arbitrary", "arbitrary"),
        name="moe_gate_up",
    )(plan.tile_e, plan.tile_new, plan.tile_ok, xs, w_gate, w_up)


def _down_kernel(te_ref, new_ref, ok_ref, x_ref, w_ref, o_ref, wb_ref):
    t = pl.program_id(1)

    @pl.when(new_ref[t] == 1)
    def _():
        wb_ref[...] = w_ref[...].astype(_BF16)

    @pl.when(ok_ref[t] == 1)
    def _():
        y = jnp.dot(x_ref[...], wb_ref[...], preferred_element_type=_F32)
        _store_slabs(o_ref, _pack_bf16_pairs(y))

    @pl.when(ok_ref[t] == 0)
    def _():
        o_ref[...] = jnp.zeros_like(o_ref)


def _down_block(d):
    return _pick(d, (2048, 1024, 512, 256))


def _down(hm, w_down, l, plan, tm):
    rows, de = hm.shape
    d = w_down.shape[3]
    bn = _down_block(d)
    k = bn // 2 // _LANES
    n_tiles = rows // tm
    return pl.pallas_call(
        _down_kernel,
        out_shape=jax.ShapeDtypeStruct((d // bn, rows * k, _LANES), _U32),
        grid_spec=pltpu.PrefetchScalarGridSpec(
            num_scalar_prefetch=3,
            grid=(d // bn, n_tiles),
            in_specs=[
                pl.BlockSpec((tm, de), lambda j, t, te, new, ok: (t, 0)),
                pl.BlockSpec((None, None, de, bn), lambda j, t, te, new, ok: (l, te[t], 0, j)),
            ],
            out_specs=pl.BlockSpec((None, tm * k, _LANES), lambda j, t, te, new, ok: (j, t, 0)),
            scratch_shapes=[pltpu.VMEM((de, bn), _BF16)],
        ),
        compiler_params=_cparams("arbitrary", "arbitrary"),
        name="moe_down",
    )(plan.tile_e, plan.tile_new, plan.tile_ok, hm, w_down)


def _combine_kernel(pos_ref, ok_ref, y_hbm, x_ref, g_ref, w_ref, gn_ref, sc_ref, sh_ref, o_ref, h_ref,
                    buf0, buf1, sem, *, bt, n, bn, final):
    t = pl.program_id(0)
    bufs = (buf0, buf1)
    nj = y_hbm.shape[0]
    k = bn // 2 // _LANES

    def slab_copy(slot, c, j, r, p):
        return pltpu.make_async_copy(y_hbm.at[j, pl.ds(p * k, k)], bufs[slot].at[c, j, pl.ds(r * k, k)],
                                     sem.at[slot])

    def start_tile(tile, slot):
        base = tile * bt

        def body(r, carry):
            for c in range(2):
                p = pos_ref[c * n + base + r]
                for j in range(nj):
                    slab_copy(slot, c, j, r, p).start(priority=(c * nj + j) % _DMA_QUEUES)
            return carry

        lax.fori_loop(0, bt, body, 0, unroll=_DMA_UNROLL // 2)

    def wait_tile(slot):
        def body(r, carry):
            for c in range(2):
                for j in range(nj):
                    slab_copy(slot, c, j, r, 0).wait()
            return carry

        lax.fori_loop(0, bt, body, 0, unroll=_DMA_UNROLL // 2)

    def consume(slot):
        w0, w1 = w_ref[:, 0:1], w_ref[:, 1:2]
        g = g_ref[0]
        for j in range(nj):
            for s in range(k):
                lo0, hi0 = _unpack_bf16_pairs(_load_slab_col(bufs[slot].at[0, j], s, bt, k))
                lo1, hi1 = _unpack_bf16_pairs(_load_slab_col(bufs[slot].at[1, j], s, bt, k))
                for c0, y in ((j * bn + s * _LANES, w0 * lo0 + w1 * lo1),
                              (j * bn + bn // 2 + s * _LANES, w0 * hi0 + w1 * hi1)):
                    o_ref[:, c0:c0 + _LANES] = x_ref[:, c0:c0 + _LANES] + g[:, c0:c0 + _LANES] * y
        xn = _rms(o_ref[...], gn_ref[...])
        if final:
            h_ref[...] = xn
        else:
            h_ref[...] = (xn * (1.0 + sc_ref[0]) + sh_ref[0]).astype(h_ref.dtype)

    _prefetched_tiles(t, pl.num_programs(0), ok_ref, start_tile, wait_tile, consume)


def _combine(y, pos, wts_t, x, mod, l, g_next, l_next, cfg, *, final):
    n, d = x.shape
    bt = _row_tile(cfg, (256, 128))
    bn = _down_block(d)
    nj, _, lanes = y.shape
    n_t = n // bt
    buf = pltpu.VMEM((2, nj, bt * (bn // 2 // lanes), lanes), _U32)
    grp = lambda i, pos, ok: _group_of_tile(i, bt, cfg)
    row = pl.BlockSpec((bt, d), lambda i, pos, ok: (i, 0))
    return pl.pallas_call(
        functools.partial(_combine_kernel, bt=bt, n=n, bn=bn, final=final),
        out_shape=(jax.ShapeDtypeStruct((n, d), _F32), jax.ShapeDtypeStruct((n, d), _F32 if final else _BF16)),
        grid_spec=pltpu.PrefetchScalarGridSpec(
            num_scalar_prefetch=2,
            grid=(n_t,),
            in_specs=[
                pl.BlockSpec(memory_space=pl.ANY),
                row,
                _mod_spec(l, _GATE2, d, grp),
                pl.BlockSpec((bt, 2), lambda i, pos, ok: (i, 0)),
                _layer_spec(g_next, l_next),
                _mod_spec(l_next, _SCALE1, d, grp),
                _mod_spec(l_next, _SHIFT1, d, grp),
            ],
            out_specs=(row, row),
            scratch_shapes=[buf, buf, pltpu.SemaphoreType.DMA((2,))],
        ),
        compiler_params=_cparams("arbitrary"),
        name="moe_combine",
    )(pos, jnp.ones((n_t,), jnp.int32), y, x, mod, wts_t, g_next, mod, mod)


def _rope_tables(cfg):
    t = cfg.s_s
    pos = jnp.arange(t, dtype=jnp.int32)
    row = (pos // _GRID_W).astype(_F32)
    col = (pos % _GRID_W).astype(_F32)

    def angles(rot_dim):
        quarter = rot_dim // 4
        inv_freq = _ROPE_THETA ** (-jnp.arange(quarter, dtype=_F32) / quarter)
        return jnp.concatenate([row[:, None] * inv_freq, col[:, None] * inv_freq], axis=-1)

    def stack(prompt_val, lat):
        lat = jnp.tile(lat, (cfg.b_s, 1))
        return jnp.concatenate([jnp.full((cfg.n_p, lat.shape[1]), prompt_val, _F32), lat], axis=0)

    ang_a, ang_b = angles(cfg.rope_a), angles(cfg.hd_b)
    cos_a, sin_a = jnp.cos(ang_a), jnp.sin(ang_a)
    cos_b, sin_b = jnp.cos(ang_b), jnp.sin(ang_b)
    return (stack(1.0, jnp.concatenate([cos_a, cos_a], -1)), stack(0.0, jnp.concatenate([sin_a, sin_a], -1)),
            stack(1.0, jnp.concatenate([cos_b, cos_b], -1)), stack(0.0, jnp.concatenate([-sin_b, sin_b], -1)))


def _keys_with_cache(own, cache, cfg):
    parts = []
    for b in range(cfg.b_s):
        parts += [cache[b].astype(own.dtype), own[cfg.n_p + b * cfg.s_s:cfg.n_p + (b + 1) * cfg.s_s]]
    parts.append(own[:cfg.n_p])
    return jnp.concatenate(parts, axis=0)


def _make_cfg(x_prompt, x_sample, cache_mla_ckv, cache_mla_krope, cache_gqa_k, w_ada, w_in, g_q_lat,
              w_uq, w_ukv, w_dw, w_router, w_gate):
    b, s, d = x_prompt.shape
    b_s, s_s, _ = x_sample.shape
    past, kv_lora = cache_mla_ckv.shape[2:]
    rope_a = cache_mla_krope.shape[3]
    kv_b, hd_b = cache_gqa_k.shape[3:]
    q_lora = g_q_lat.shape[1]
    conv_w, c_conv = w_dw.shape[1:]
    h_b = (w_in.shape[2] - q_lora - kv_lora - rope_a - 2 * kv_b * hd_b - 2 * c_conv) // hd_b
    hv = d - h_b * hd_b - c_conv
    h_a = (w_uq.shape[2] - w_ukv.shape[2] + hv) // rope_a
    v_a = hv // h_a
    nope_a = w_ukv.shape[2] // h_a - v_a
    return _Cfg(d=d, n_p=b * s, s_p=s, b_s=b_s, s_s=s_s, past=past, depth=w_ada.shape[0], q_lora=q_lora,
                kv_lora=kv_lora, rope_a=rope_a, nope_a=nope_a, v_a=v_a, h_a=h_a, h_b=h_b, kv_b=kv_b,
                hd_b=hd_b, c_conv=c_conv, conv_w=conv_w, n_exp=w_router.shape[1], d_exp=w_gate.shape[3])


def kernel(x_prompt, x_sample, c, c_ctx, cache_mla_ckv, cache_mla_krope, cache_gqa_k, cache_gqa_v, w_ada, b_ada, g_norm1, g_norm2, w_in, g_q_lat, g_kv_lat, w_uq, w_ukv, g_q_b, g_k_b, w_dw, b_dw, g_conv_ln, b_conv_ln, w_out, w_router, router_bias, w_gate, w_up, w_down, g_final):
    cfg = _make_cfg(x_prompt, x_sample, cache_mla_ckv, cache_mla_krope, cache_gqa_k, w_ada, w_in, g_q_lat,
                    w_uq, w_ukv, w_dw, w_router, w_gate)
    d, n_p, depth = cfg.d, cfg.n_p, cfg.depth
    batch, seq = x_prompt.shape[:2]
    kvw = cfg.kv_b * cfg.hd_b
    assert cfg.v_a == cfg.hd_b and 1 + cfg.b_s <= _MOD_ROWS

    x = jnp.concatenate([x_prompt.reshape(n_p, d), x_sample.reshape(cfg.n_s, d)], axis=0)

    m = jnp.concatenate([c_ctx[None, :], c, jnp.zeros((_MOD_ROWS - 1 - cfg.b_s, d), _F32)], axis=0)
    mod = _ada(jax.nn.silu(m).astype(_BF16), w_ada, b_ada).reshape(depth, _MOD_ROWS, 6, 1, d)

    vec = lambda a: a.reshape(depth, 1, a.shape[-1])
    g_norm1, g_norm2 = vec(g_norm1), vec(g_norm2)
    gains = [vec(g_q_lat), vec(g_kv_lat), vec(g_q_b), vec(g_k_b)]
    b_dw, g_conv_ln, b_conv_ln = vec(b_dw), vec(g_conv_ln), vec(b_conv_ln)
    w_in_r = _reorder_w_in(w_in, cfg)
    w_uq_r = _reorder_w_uq(w_uq, cfg)
    w_ukv_b = w_ukv.astype(_BF16)
    tabs = _rope_tables(cfg)
    tm = _pick(2 * cfg.n, (512, 256, 128))
    bq_s = _pick(cfg.s_s, (256, 128))

    ckvs, kpes, kbs, vbs = [], [], [], []
    h = _norm_mod(x, g_norm1, mod, 0, cfg)
    for l in range(depth):
        p =_matmul(h, w_in_r, l, _F32, "in_proj")
        q_a, ckv, kpe, q_b, k_b, v_b, k_b16, v_b16, glu = _post_projection(p, w_uq_r, gains, l, tabs, cfg)
        ckvs.append(ckv[:n_p])
        kpes.append(kpe[:n_p])
        kbs.append(k_b[:n_p])
        vbs.append(v_b[:n_p])

        k_a, v_a = _kv_a(_keys_with_cache(ckv, cache_mla_ckv[:, l], cfg),
                         _keys_with_cache(kpe, cache_mla_krope[:, l], cfg), w_ukv_b, l, cfg)
        common_a = dict(group=1, dv=cfg.v_a, kv_head_major=True)
        a_p = _attention(q_a, k_a, v_a, n_seq=batch, tq=seq, tk=seq, q_row0=0, k_row0=cfg.b_s * cfg.tk_s,
                         hb=cfg.h_a, bq=seq, name="attn_a_prompt", **common_a)
        a_s = _attention(q_a, k_a, v_a, n_seq=cfg.b_s, tq=cfg.s_s, tk=cfg.tk_s, q_row0=n_p, k_row0=0,
                         hb=2 if cfg.h_a % 2 == 0 else 1, bq=bq_s, name="attn_a_latent", **common_a)
        k_bk = _keys_with_cache(k_b16, cache_gqa_k[:, l].reshape(cfg.b_s, cfg.past, kvw), cfg)
        v_bk = _keys_with_cache(v_b16, cache_gqa_v[:, l].reshape(cfg.b_s, cfg.past, kvw), cfg)
        grp = cfg.h_b // cfg.kv_b
        common_b = dict(group=grp, dv=cfg.hd_b, kv_head_major=False)
        b_p = _attention(q_b, k_bk, v_bk, n_seq=batch, tq=seq, tk=seq, q_row0=0, k_row0=cfg.b_s * cfg.tk_s,
                         hb=cfg.h_b, bq=seq, name="attn_b_prompt", **common_b)
        b_s = _attention(q_b, k_bk, v_bk, n_seq=cfg.b_s, tq=cfg.s_s, tk=cfg.tk_s, q_row0=n_p, k_row0=0,
                         hb=grp, bq=bq_s, name="attn_b_latent", **common_b)
        out_c = _conv_module(glu, w_dw, b_dw, g_conv_ln, b_conv_ln, l, cfg)

        x = _out_proj(jnp.concatenate([a_p, a_s], axis=0), jnp.concatenate([b_p, b_s], axis=0), out_c,
                      w_out, x, mod, l, cfg)

        h2, idx, wts = _norm_router(x, g_norm2, mod, l, w_router, router_bias, cfg)
        plan, n_tiles = _route_plan(idx, cfg.n_exp, tm)
        xs = _dispatch(h2, plan, tm, cfg)
        hm = _gate_up(xs, w_gate, w_up, l, plan, tm)
        y = _down(hm, w_down, l, plan, tm)
        if l + 1 < depth:
            x, h = _combine(y, plan.pos, wts.T, x, mod, l, g_norm1, l + 1, cfg, final=False)
        else:
            x, y_all = _combine(y, plan.pos, wts.T, x, mod, l, g_final.reshape(1, 1, d), 0, cfg, final=True)

    stack = lambda parts, tail: jnp.stack([a.reshape((batch, seq) + tail) for a in parts], axis=1)
    return (y_all[:n_p].reshape(batch, seq, d),
            y_all[n_p:].reshape(cfg.b_s, cfg.s_s, d),
            stack(ckvs, (cfg.kv_lora,)),
            stack(kpes, (cfg.rope_a,)),
            stack(kbs, (cfg.kv_b, cfg.hd_b)),
            stack(vbs, (cfg.kv_b, cfg.hd_b)))
```

```python
import functools
from typing import NamedTuple

import jax
import jax.numpy as jnp
from jax import lax
from jax.experimental import pallas as pl
from jax.experimental.pallas import tpu as pltpu

_F32 = jnp.float32
_BF16 = jnp.bfloat16
_U32 = jnp.uint32

_GRID_W = 64
_N_GROUPS = 4
_ROPE_THETA = 10000.0
_EPS = 1e-6
_LOG2E = 1.4426950408889634

_V7X_VMEM_BYTES = 64 * 1024 * 1024
_VMEM_LIMIT = _V7X_VMEM_BYTES - 8 * 1024 * 1024
_LANES = 128
_SUBLANES = 8
_MOD_ROWS = 16
_SHIFT1, _SCALE1, _GATE1, _SHIFT2, _SCALE2, _GATE2 = range(6)


class _Cfg(NamedTuple):
    d: int
    n_p: int
    s_p: int
    b_s: int
    s_s: int
    past: int
    depth: int
    q_lora: int
    kv_lora: int
    rope_a: int
    nope_a: int
    v_a: int
    h_a: int
    h_b: int
    kv_b: int
    hd_b: int
    c_conv: int
    conv_w: int
    n_exp: int
    d_exp: int

    @property
    def n_s(self):
        return self.b_s * self.s_s

    @property
    def n(self):
        return self.n_p + self.n_s

    @property
    def tk_s(self):
        return self.past + self.s_s


def _pick(n, candidates):
    for c in candidates:
        if n % c == 0:
            return c
    raise ValueError(f"no tile in {candidates} divides {n}")


def _row_tile(cfg, candidates):
    for c in candidates:
        if cfg.n_p % c == 0 and cfg.s_s % c == 0:
            return c
    raise ValueError(f"no row tile in {candidates}")


def _cparams(*sem):
    return pltpu.CompilerParams(dimension_semantics=sem, vmem_limit_bytes=_VMEM_LIMIT)


def _group_of_tile(i, bm, cfg):
    r0 = i * bm
    return jnp.where(r0 < cfg.n_p, 0, 1 + (r0 - cfg.n_p) // cfg.s_s)


def _mod_spec(l, which, width, group_fn, col_fn=None):
    def index(*ids):
        return (l, group_fn(*ids), which, 0, 0 if col_fn is None else col_fn(*ids))
    return pl.BlockSpec((None, 1, None, 1, width), index)


def _layer_spec(a, l):
    nd = a.ndim - 1
    return pl.BlockSpec((None,) + a.shape[1:], lambda *ids: (l,) + (0,) * nd)


def _pack_bf16_pairs(x):
    w = x.shape[1] // 2
    r = x.astype(_BF16).astype(_F32)
    lo = lax.bitcast_convert_type(r[:, :w], _U32) >> 16
    hi = lax.bitcast_convert_type(r[:, w:], _U32) & _U32(0xFFFF0000)
    return lo | hi


def _unpack_bf16_pairs(p):
    lo = lax.bitcast_convert_type(p << 16, _F32)
    hi = lax.bitcast_convert_type(p & _U32(0xFFFF0000), _F32)
    return lo, hi


def _store_slabs(o_ref, x):
    rows, w = x.shape
    k = w // _LANES
    for s in range(k):
        o_ref[pl.ds(s, rows, stride=k), :] = x[:, s * _LANES:(s + 1) * _LANES]


def _load_slab_col(ref, s, rows, k):
    return ref[pl.ds(s, rows, stride=k), :]


def _ada_kernel(s_ref, w_ref, b_ref, o_ref):
    w = w_ref[0].astype(_BF16)
    o_ref[0] = jnp.dot(s_ref[...], w, preferred_element_type=_F32) + b_ref[0]


def _ada(sm, w_ada, b_ada):
    depth, d, n6 = w_ada.shape
    bn = _pick(n6, (512, 256, 128))
    return pl.pallas_call(
        _ada_kernel,
        out_shape=jax.ShapeDtypeStruct((depth, _MOD_ROWS, n6), _F32),
        grid=(depth, n6 // bn),
        in_specs=[
            pl.BlockSpec((_MOD_ROWS, d), lambda l, j: (0, 0)),
            pl.BlockSpec((1, d, bn), lambda l, j: (l, 0, j)),
            pl.BlockSpec((1, 1, bn), lambda l, j: (l, 0, j)),
        ],
        out_specs=pl.BlockSpec((1, _MOD_ROWS, bn), lambda l, j: (l, 0, j)),
        compiler_params=_cparams("arbitrary", "arbitrary"),
        name="ada",
    )(sm, w_ada, b_ada.reshape(depth, 1, n6))


def _rms(x, g):
    return x * lax.rsqrt(jnp.mean(x * x, axis=-1, keepdims=True) + _EPS) * g


def _norm_mod_kernel(x_ref, g_ref, sc_ref, sh_ref, o_ref):
    h = _rms(x_ref[...], g_ref[...]) * (1.0 + sc_ref[0]) + sh_ref[0]
    o_ref[...] = h.astype(o_ref.dtype)


def _norm_mod(x, g, mod, l, cfg):
    n, d = x.shape
    bm = _row_tile(cfg, (256, 128))
    grp = lambda i: _group_of_tile(i, bm, cfg)
    return pl.pallas_call(
        _norm_mod_kernel,
        out_shape=jax.ShapeDtypeStruct((n, d), _BF16),
        grid=(n // bm,),
        in_specs=[
            pl.BlockSpec((bm, d), lambda i: (i, 0)),
            _layer_spec(g, l),
            _mod_spec(l, _SCALE1, d, grp),
            _mod_spec(l, _SHIFT1, d, grp),
        ],
        out_specs=pl.BlockSpec((bm, d), lambda i: (i, 0)),
        compiler_params=_cparams("arbitrary"),
        name="norm_mod",
    )(x, g, mod, mod)


def _mm_kernel(x_ref, w_ref, o_ref):
    o_ref[...] = jnp.dot(x_ref[...], w_ref[...], preferred_element_type=_F32).astype(o_ref.dtype)


def _matmul(x, w, l, out_dtype, name):
    m, k = x.shape
    n = w.shape[2]
    bm = _pick(m, (1024, 512, 256, 128))
    bn = _pick(n, (512, 256, 128))
    return pl.pallas_call(
        _mm_kernel,
        out_shape=jax.ShapeDtypeStruct((m, n), out_dtype),
        grid=(n // bn, m // bm),
        in_specs=[pl.BlockSpec((bm, k), lambda j, i: (i, 0)),
                  pl.BlockSpec((None, k, bn), lambda j, i: (l, 0, j))],
        out_specs=pl.BlockSpec((bm, bn), lambda j, i: (i, j)),
        compiler_params=_cparams("arbitrary", "arbitrary"),
        name=name,
    )(x, w)


class _InCols(NamedTuple):
    q_lat: int
    kv_lat: int
    q_b: int
    k_b: int
    v_b: int
    u_c: int
    k_pe: int
    k_pe_rot: int
    width: int


def _in_cols(cfg):
    offs, acc = [], 0
    for w in (cfg.q_lora, cfg.kv_lora, cfg.h_b * cfg.hd_b, cfg.kv_b * cfg.hd_b, cfg.kv_b * cfg.hd_b,
              2 * cfg.c_conv, cfg.rope_a, cfg.rope_a):
        offs.append(acc)
        acc += w
    width = -(-acc // 512) * 512
    return _InCols(*offs, width)


def _rotate_half_cols(w):
    half = w.shape[-1] // 2
    return jnp.concatenate([-w[..., half:], w[..., :half]], axis=-1)


def _reorder_w_in_kernel(w_ref, o_ref, *, o, r):
    w = w_ref[...]
    src_w = w.shape[1]
    k_pe = w[:, o:o + r]
    o_ref[:, :o] = w[:, :o].astype(o_ref.dtype)
    o_ref[:, o:src_w - r] = w[:, o + r:].astype(o_ref.dtype)
    tail = [k_pe, _rotate_half_cols(k_pe)]
    pad = o_ref.shape[1] - src_w - r
    if pad:
        tail.append(jnp.zeros((w.shape[0], pad), w.dtype))
    o_ref[:, src_w - r:] = jnp.concatenate(tail, axis=-1).astype(o_ref.dtype)


def _reorder_w_in(w_in, cfg):
    depth, k, src_w = w_in.shape
    cols = _in_cols(cfg)
    bk = _pick(k, (256, 128))
    return pl.pallas_call(
        functools.partial(_reorder_w_in_kernel, o=cfg.q_lora + cfg.kv_lora, r=cfg.rope_a),
        out_shape=jax.ShapeDtypeStruct((depth, k, cols.width), _BF16),
        grid=(depth, k // bk),
        in_specs=[pl.BlockSpec((None, bk, src_w), lambda l, i: (l, i, 0))],
        out_specs=pl.BlockSpec((None, bk, cols.width), lambda l, i: (l, i, 0)),
        compiler_params=_cparams("arbitrary", "arbitrary"),
        name="reorder_w_in",
    )(w_in)


def _reorder_w_uq(w_uq, cfg):
    depth, ql, _ = w_uq.shape
    w = w_uq.reshape(depth, ql, cfg.h_a, cfg.nope_a + cfg.rope_a)
    nope = w[..., :cfg.nope_a].reshape(depth, ql, -1)
    pe = w[..., cfg.nope_a:]
    rot = _rotate_half_cols(pe).reshape(depth, ql, -1)
    return jnp.concatenate([nope, pe.reshape(depth, ql, -1), rot], axis=-1).astype(_BF16)


def _post_kernel(p_ref, wuq_ref, gq_ref, gkv_ref, gqb_ref, gkb_ref, ca_ref, sa_ref, cb_ref, sb_ref,
                 qa_ref, ckv_ref, kpe_ref, qb_ref, kb_ref, vb_ref, kb16_ref, vb16_ref, glu_ref, *, cfg, cols):
    ra, hd = cfg.rope_a, cfg.hd_b
    ca, sa, cb, sb = ca_ref[...], sa_ref[...], cb_ref[...], sb_ref[...]

    qn = _rms(p_ref[:, cols.q_lat:cols.q_lat + cfg.q_lora], gq_ref[...]).astype(_BF16)
    qa = jnp.dot(qn, wuq_ref[...], preferred_element_type=_F32)
    pe0 = cfg.h_a * cfg.nope_a
    rot0 = pe0 + cfg.h_a * ra
    scale_a = _LOG2E * (cfg.nope_a + ra) ** -0.5
    for h in range(cfg.h_a):
        nope = qa[:, h * cfg.nope_a:(h + 1) * cfg.nope_a]
        pe = qa[:, pe0 + h * ra:pe0 + (h + 1) * ra] * ca + qa[:, rot0 + h * ra:rot0 + (h + 1) * ra] * sa
        qa_ref[h] = (jnp.concatenate([nope, pe], axis=-1) * scale_a).astype(qa_ref.dtype)

    ckv_ref[...] = _rms(p_ref[:, cols.kv_lat:cols.kv_lat + cfg.kv_lora], gkv_ref[...])
    kpe_ref[...] = p_ref[:, cols.k_pe:cols.k_pe + ra] * ca + p_ref[:, cols.k_pe_rot:cols.k_pe_rot + ra] * sa

    def head_norm_rope(x, g):
        y = _rms(x, g)
        return y * cb + pltpu.roll(y, hd // 2, axis=1) * sb

    scale_b = _LOG2E * hd ** -0.5
    for h in range(cfg.h_b):
        x = p_ref[:, cols.q_b + h * hd:cols.q_b + (h + 1) * hd]
        qb_ref[h] = (head_norm_rope(x, gqb_ref[...]) * scale_b).astype(qb_ref.dtype)
    for h in range(cfg.kv_b):
        x = p_ref[:, cols.k_b + h * hd:cols.k_b + (h + 1) * hd]
        kh = head_norm_rope(x, gkb_ref[...])
        kb_ref[:, h * hd:(h + 1) * hd] = kh
        kb16_ref[:, h * hd:(h + 1) * hd] = kh.astype(kb16_ref.dtype)
    vb = p_ref[:, cols.v_b:cols.v_b + cfg.kv_b * hd]
    vb_ref[...] = vb
    vb16_ref[...] = vb.astype(vb16_ref.dtype)

    a = p_ref[:, cols.u_c:cols.u_c + cfg.c_conv]
    gate = p_ref[:, cols.u_c + cfg.c_conv:cols.u_c + 2 * cfg.c_conv]
    glu_ref[...] = a * jax.nn.sigmoid(gate)


def _post_projection(p, w_uq_r, gains, l, tabs, cfg):
    n = cfg.n
    cols = _in_cols(cfg)
    bm = _pick(n, (256, 128))
    dk_a = cfg.nope_a + cfg.rope_a
    kvw = cfg.kv_b * cfg.hd_b
    row = lambda w: pl.BlockSpec((bm, w), lambda i: (i, 0))
    out_shape = (
        jax.ShapeDtypeStruct((cfg.h_a, n, dk_a), _BF16),
        jax.ShapeDtypeStruct((n, cfg.kv_lora), _F32),
        jax.ShapeDtypeStruct((n, cfg.rope_a), _F32),
        jax.ShapeDtypeStruct((cfg.h_b, n, cfg.hd_b), _BF16),
        jax.ShapeDtypeStruct((n, kvw), _F32),
        jax.ShapeDtypeStruct((n, kvw), _F32),
        jax.ShapeDtypeStruct((n, kvw), _BF16),
        jax.ShapeDtypeStruct((n, kvw), _BF16),
        jax.ShapeDtypeStruct((n, cfg.c_conv), _F32),
    )
    out_specs = (
        pl.BlockSpec((cfg.h_a, bm, dk_a), lambda i: (0, i, 0)),
        row(cfg.kv_lora),
        row(cfg.rope_a),
        pl.BlockSpec((cfg.h_b, bm, cfg.hd_b), lambda i: (0, i, 0)),
        row(kvw),
        row(kvw),
        row(kvw),
        row(kvw),
        row(cfg.c_conv),
    )
    return pl.pallas_call(
        functools.partial(_post_kernel, cfg=cfg, cols=cols),
        out_shape=out_shape,
        grid=(n // bm,),
        in_specs=[row(cols.width), _layer_spec(w_uq_r, l)] + [_layer_spec(g, l) for g in gains]
        + [row(cfg.rope_a), row(cfg.rope_a), row(cfg.hd_b), row(cfg.hd_b)],
        out_specs=out_specs,
        compiler_params=_cparams("arbitrary"),
        name="post_projection",
    )(p, w_uq_r, *gains, *tabs)


def _kv_a_kernel(ckv_ref, kpe_ref, w_ref, k_ref, v_ref, *, cfg):
    kv = jnp.dot(ckv_ref[...].astype(_BF16), w_ref[...], preferred_element_type=_F32)
    kpe = kpe_ref[...]
    hw = cfg.nope_a + cfg.v_a
    for h in range(cfg.h_a):
        k_ref[h] = jnp.concatenate([kv[:, h * hw:h * hw + cfg.nope_a], kpe], axis=-1).astype(k_ref.dtype)
        v_ref[h] = kv[:, h * hw + cfg.nope_a:(h + 1) * hw].astype(v_ref.dtype)


def _kv_a(ckv_keys, kpe_keys, w_ukv, l, cfg):
    t = ckv_keys.shape[0]
    bm = _pick(t, (256, 128))
    dk = cfg.nope_a + cfg.rope_a
    return pl.pallas_call(
        functools.partial(_kv_a_kernel, cfg=cfg),
        out_shape=(jax.ShapeDtypeStruct((cfg.h_a, t, dk), _BF16),
                   jax.ShapeDtypeStruct((cfg.h_a, t, cfg.v_a), _BF16)),
        grid=(t // bm,),
        in_specs=[
            pl.BlockSpec((bm, cfg.kv_lora), lambda i: (i, 0)),
            pl.BlockSpec((bm, cfg.rope_a), lambda i: (i, 0)),
            _layer_spec(w_ukv, l),
        ],
        out_specs=(pl.BlockSpec((cfg.h_a, bm, dk), lambda i: (0, i, 0)),
                   pl.BlockSpec((cfg.h_a, bm, cfg.v_a), lambda i: (0, i, 0))),
        compiler_params=_cparams("arbitrary"),
        name="kv_a",
    )(ckv_keys, kpe_keys, w_ukv)


def _attn_kernel(q_ref, k_ref, v_ref, o_ref, *, hb, group, dv, kv_head_major):
    for j in range(hb):
        kvj = j // group
        q = q_ref[j]
        if kv_head_major:
            k, v = k_ref[kvj], v_ref[kvj]
        else:
            k, v = k_ref[:, kvj * dv:(kvj + 1) * dv], v_ref[:, kvj * dv:(kvj + 1) * dv]
        s = lax.dot_general(q, k, (((1,), (1,)), ((), ())), preferred_element_type=_F32)
        m = jnp.max(s, axis=-1, keepdims=True)
        p = jnp.exp2(s - m)
        l = jnp.sum(p, axis=-1, keepdims=True)
        o = jnp.dot(p.astype(_BF16), v, preferred_element_type=_F32) / l
        o_ref[:, j * dv:(j + 1) * dv] = o.astype(o_ref.dtype)


def _attention(q, k, v, *, n_seq, tq, tk, q_row0, k_row0, hb, group, dv, bq, kv_head_major, name):
    hq, _, dk = q.shape
    hkb = hb // group
    nq = tq // bq
    qb0, kb0 = q_row0 // bq, k_row0 // tk
    assert q_row0 % bq == 0 and k_row0 % tk == 0 and hq % hb == 0 and hb % group == 0
    if kv_head_major:
        k_spec = pl.BlockSpec((hkb, tk, k.shape[2]), lambda s, h, i: (h, kb0 + s, 0))
        v_spec = pl.BlockSpec((hkb, tk, v.shape[2]), lambda s, h, i: (h, kb0 + s, 0))
    else:
        k_spec = pl.BlockSpec((tk, hkb * dv), lambda s, h, i: (kb0 + s, h))
        v_spec = pl.BlockSpec((tk, hkb * dv), lambda s, h, i: (kb0 + s, h))
    return pl.pallas_call(
        functools.partial(_attn_kernel, hb=hb, group=group, dv=dv, kv_head_major=kv_head_major),
        out_shape=jax.ShapeDtypeStruct((n_seq * tq, hq * dv), _BF16),
        grid=(n_seq, hq // hb, nq),
        in_specs=[pl.BlockSpec((hb, bq, dk), lambda s, h, i: (h, qb0 + s * nq + i, 0)), k_spec, v_spec],
        out_specs=pl.BlockSpec((bq, hb * dv), lambda s, h, i: (s * nq + i, h)),
        compiler_params=_cparams("arbitrary", "arbitrary", "arbitrary"),
        name=name,
    )(q, k, v)


_HALO = 16
_CONV_ROWS = 32


def _conv_kernel(prev_ref, cur_ref, next_ref, w_ref, bdw_ref, g_ref, b_ref, o_ref, buf_ref, sh_ref, *, cfg, bt):
    i = pl.program_id(0)
    n_pt = cfg.n_p // bt
    per_seq = cfg.s_s // bt
    j = jnp.maximum(i - n_pt, 0) % per_seq
    lat = i >= n_pt
    p_per = cfg.s_p // bt
    jp = i % p_per
    has_prev = jnp.where(lat, j > 0, jp > 0)
    has_next = jnp.where(lat, j < per_seq - 1, jp < p_per - 1)
    buf_ref[0:_HALO, :] = jnp.where(has_prev, prev_ref[...], 0.0)
    buf_ref[_HALO:_HALO + bt, :] = cur_ref[...]
    buf_ref[_HALO + bt:2 * _HALO + bt, :] = jnp.where(has_next, next_ref[...], 0.0)
    span = sh_ref.shape[1]
    for s in range(_SUBLANES):
        sh_ref[s] = buf_ref[s:s + span, :]
    half = cfg.conv_w // 2
    for r in range(0, bt, _CONV_ROWS):
        acc = jnp.zeros((_CONV_ROWS, cfg.c_conv), _F32)
        for t in range(cfg.conv_w):
            lo = _HALO + r + t - half
            s = lo % _SUBLANES
            acc = acc + sh_ref[s, lo - s:lo - s + _CONV_ROWS, :] * w_ref[t:t + 1, :]
        y = acc + bdw_ref[...]
        mu = jnp.mean(y, axis=-1, keepdims=True)
        yc = y - mu
        var = jnp.mean(yc * yc, axis=-1, keepdims=True)
        z = yc * lax.rsqrt(var + _EPS) * g_ref[...] + b_ref[...]
        o_ref[r:r + _CONV_ROWS, :] = (z * jax.nn.sigmoid(z)).astype(o_ref.dtype)


def _conv_module(v, w_dw, b_dw, g_ln, b_ln, l, cfg):
    n, c = v.shape
    bt = _pick(cfg.s_p, (256, 128))
    assert cfg.s_s % bt == 0 and cfg.conv_w // 2 < _HALO and bt % _CONV_ROWS == 0
    hb = bt // _HALO
    last = n // _HALO - 1
    span = bt + 2 * _HALO - _SUBLANES
    return pl.pallas_call(
        functools.partial(_conv_kernel, cfg=cfg, bt=bt),
        out_shape=jax.ShapeDtypeStruct((n, c), _BF16),
        grid=(n // bt,),
        in_specs=[
            pl.BlockSpec((_HALO, c), lambda i: (jnp.maximum(i * hb - 1, 0), 0)),
            pl.BlockSpec((bt, c), lambda i: (i, 0)),
            pl.BlockSpec((_HALO, c), lambda i: (jnp.minimum((i + 1) * hb, last), 0)),
            _layer_spec(w_dw, l), _layer_spec(b_dw, l), _layer_spec(g_ln, l), _layer_spec(b_ln, l),
        ],
        out_specs=pl.BlockSpec((bt, c), lambda i: (i, 0)),
        scratch_shapes=[pltpu.VMEM((bt + 2 * _HALO, c), _F32), pltpu.VMEM((_SUBLANES, span, c), _F32)],
        compiler_params=_cparams("arbitrary"),
        name="conv_module",
    )(v, v, v, w_dw, b_dw, g_ln, b_ln)


def _out_proj_kernel(a_ref, b_ref, c_ref, w_ref, x_ref, g_ref, o_ref, wb_ref, *, wa, wbw):
    @pl.when(pl.program_id(1) == 0)
    def _():
        wb_ref[...] = w_ref[...].astype(_BF16)

    acc = jnp.dot(a_ref[...], wb_ref[0:wa, :], preferred_element_type=_F32)
    acc = acc + jnp.dot(b_ref[...], wb_ref[wa:wa + wbw, :], preferred_element_type=_F32)
    acc = acc + jnp.dot(c_ref[...], wb_ref[wa + wbw:, :], preferred_element_type=_F32)
    o_ref[...] = x_ref[...] + g_ref[0] * acc


def _out_proj(out_a, out_b, out_c, w_out, x, mod, l, cfg):
    n, d = x.shape
    wa, wbw, wc = out_a.shape[1], out_b.shape[1], out_c.shape[1]
    bm = _row_tile(cfg, (1024, 512, 256, 128))
    bn = _pick(d, (512, 256, 128))
    return pl.pallas_call(
        functools.partial(_out_proj_kernel, wa=wa, wbw=wbw),
        out_shape=jax.ShapeDtypeStruct((n, d), _F32),
        grid=(d // bn, n // bm),
        in_specs=[
            pl.BlockSpec((bm, wa), lambda j, i: (i, 0)),
            pl.BlockSpec((bm, wbw), lambda j, i: (i, 0)),
            pl.BlockSpec((bm, wc), lambda j, i: (i, 0)),
            pl.BlockSpec((None, d, bn), lambda j, i: (l, 0, j)),
            pl.BlockSpec((bm, bn), lambda j, i: (i, j)),
            _mod_spec(l, _GATE1, bn, lambda j, i: _group_of_tile(i, bm, cfg), lambda j, i: j),
        ],
        out_specs=pl.BlockSpec((bm, bn), lambda j, i: (i, j)),
        scratch_shapes=[pltpu.VMEM((d, bn), _BF16)],
        compiler_params=_cparams("arbitrary", "arbitrary"),
        name="out_proj",
    )(out_a, out_b, out_c, w_out, x, mod)


def _norm_router_kernel(x_ref, g_ref, sc_ref, sh_ref, wr_ref, br_ref, h_ref, idx_ref, wts_ref, *, n_exp):
    h = _rms(x_ref[...], g_ref[...]) * (1.0 + sc_ref[0]) + sh_ref[0]
    _store_slabs(h_ref, _pack_bf16_pairs(h))
    logits = jnp.dot(h.astype(_BF16), wr_ref[...], preferred_element_type=_F32)
    scores = jax.nn.sigmoid(logits)
    biased = scores + br_ref[...]
    sc_t = scores.T
    bi_t = biased.T
    epg = n_exp // _N_GROUPS
    row = lambda a, e: a[e:e + 1, :]

    group_scores = []
    for g in range(_N_GROUPS):
        vals = [row(bi_t, g * epg + i) for i in range(epg)]
        best = None
        for i in range(epg):
            for j in range(i + 1, epg):
                pair = vals[i] + vals[j]
                best = pair if best is None else jnp.maximum(best, pair)
        group_scores.append(best)
    sel = jnp.zeros_like(group_scores[0], dtype=jnp.int32)
    best = group_scores[0]
    for g in range(1, _N_GROUPS):
        upd = group_scores[g] > best
        sel = jnp.where(upd, g, sel)
        best = jnp.where(upd, group_scores[g], best)

    def in_group(a, i):
        out = row(a, i)
        for g in range(1, _N_GROUPS):
            out = jnp.where(sel == g, row(a, g * epg + i), out)
        return out

    b = [in_group(bi_t, i) for i in range(epg)]
    u = [in_group(sc_t, i) for i in range(epg)]
    i1, v1, w1 = jnp.zeros_like(sel), b[0], u[0]
    for i in range(1, epg):
        upd = b[i] > v1
        i1, v1, w1 = jnp.where(upd, i, i1), jnp.where(upd, b[i], v1), jnp.where(upd, u[i], w1)
    i2 = jnp.zeros_like(sel)
    v2 = jnp.full_like(v1, -jnp.inf)
    w2 = jnp.zeros_like(w1)
    for i in range(epg):
        upd = jnp.logical_and(i1 != i, b[i] > v2)
        i2, v2, w2 = jnp.where(upd, i, i2), jnp.where(upd, b[i], v2), jnp.where(upd, u[i], w2)
    den = w1 + w2
    idx_ref[0:1, :] = sel * epg + i1
    idx_ref[1:2, :] = sel * epg + i2
    wts_ref[0:1, :] = w1 / den
    wts_ref[1:2, :] = w2 / den


def _norm_router(x, g, mod, l, w_router, router_bias, cfg):
    n, d = x.shape
    bm = _row_tile(cfg, (256, 128))
    e = cfg.n_exp
    k_slab = d // 2 // _LANES
    wr = jnp.zeros((d, _LANES), _BF16).at[:, :e].set(w_router.astype(_BF16))
    br = jnp.zeros((1, _LANES), _F32).at[0, :e].set(router_bias)
    grp = lambda i: _group_of_tile(i, bm, cfg)
    return pl.pallas_call(
        functools.partial(_norm_router_kernel, n_exp=e),
        out_shape=(jax.ShapeDtypeStruct((n * k_slab, _LANES), _U32),
                   jax.ShapeDtypeStruct((2, n), jnp.int32),
                   jax.ShapeDtypeStruct((2, n), _F32)),
        grid=(n // bm,),
        in_specs=[
            pl.BlockSpec((bm, d), lambda i: (i, 0)),
            _layer_spec(g, l),
            _mod_spec(l, _SCALE2, d, grp),
            _mod_spec(l, _SHIFT2, d, grp),
            pl.BlockSpec((d, _LANES), lambda i: (0, 0)),
            pl.BlockSpec((1, _LANES), lambda i: (0, 0)),
        ],
        out_specs=(pl.BlockSpec((bm * k_slab, _LANES), lambda i: (i, 0)),
                   pl.BlockSpec((2, bm), lambda i: (0, i)),
                   pl.BlockSpec((2, bm), lambda i: (0, i))),
        compiler_params=_cparams("arbitrary"),
        name="norm_router",
    )(x, g, mod, mod, wr, br)


class _Plan(NamedTuple):
    pos: jax.Array
    row_tok: jax.Array
    tile_e: jax.Array
    tile_new: jax.Array
    tile_ok: jax.Array


def _route_plan(idx, n_exp, tm):
    n = idx.shape[1]
    e_flat = idx.reshape(-1)
    onehot = (e_flat[:, None] == jnp.arange(n_exp, dtype=jnp.int32)[None, :]).astype(jnp.int32)
    rank = jnp.sum((jnp.cumsum(onehot, axis=0) - onehot) * onehot, axis=1)
    counts = jnp.sum(onehot, axis=0)
    tiles_per = (counts + tm - 1) // tm
    tile_end = jnp.cumsum(tiles_per)
    tile_start = tile_end - tiles_per
    pos = (tile_start[e_flat] * tm + rank).astype(jnp.int32)
    n_tiles = (2 * n) // tm + n_exp
    tok = jnp.arange(2 * n, dtype=jnp.int32) % n
    row_tok = jnp.zeros((n_tiles * tm,), jnp.int32).at[pos].set(tok)
    tid = jnp.arange(n_tiles, dtype=jnp.int32)
    total = tile_end[-1]
    ok = (tid < total).astype(jnp.int32)
    te = jnp.searchsorted(tile_end, jnp.minimum(tid, total - 1), side="right").astype(jnp.int32)
    new = jnp.concatenate([jnp.ones((1,), jnp.int32), (te[1:] != te[:-1]).astype(jnp.int32)])
    return _Plan(pos, row_tok, te, new, ok), n_tiles


_DMA_UNROLL = 8
_DMA_QUEUES = 2


def _prefetched_tiles(t, n_t, ok_ref, start_tile, wait_tile, consume):
    slot = t % 2
    nxt = jnp.minimum(t + 1, n_t - 1)
    fetch_next = jnp.logical_and(t + 1 < n_t, ok_ref[nxt] == 1)

    @pl.when(jnp.logical_and(t == 0, ok_ref[0] == 1))
    def _():
        start_tile(0, 0)

    for sl in (0, 1):
        @pl.when(jnp.logical_and(fetch_next, slot == 1 - sl))
        def _():
            start_tile(t + 1, sl)

    for sl in (0, 1):
        @pl.when(jnp.logical_and(ok_ref[t] == 1, slot == sl))
        def _():
            wait_tile(sl)
            consume(sl)


def _dispatch_kernel(tok_ref, ok_ref, h_hbm, o_ref, buf0, buf1, sem, *, tm, k):
    t = pl.program_id(0)
    bufs = (buf0, buf1)

    def slab_copy(slot, r, tok):
        return pltpu.make_async_copy(h_hbm.at[pl.ds(tok * k, k)], bufs[slot].at[pl.ds(r * k, k)], sem.at[slot])

    def start_tile(tile, slot):
        base = tile * tm

        def body(r2, c):
            for q in range(_DMA_QUEUES):
                r = _DMA_QUEUES * r2 + q
                slab_copy(slot, r, tok_ref[base + r]).start(priority=q)
            return c

        lax.fori_loop(0, tm // _DMA_QUEUES, body, 0, unroll=_DMA_UNROLL // _DMA_QUEUES)

    def wait_tile(slot):
        def body(r, c):
            slab_copy(slot, r, 0).wait()
            return c

        lax.fori_loop(0, tm, body, 0, unroll=_DMA_UNROLL)

    def consume(slot):
        half = k * _LANES
        for s in range(k):
            lo, hi = _unpack_bf16_pairs(_load_slab_col(bufs[slot], s, tm, k))
            o_ref[:, s * _LANES:(s + 1) * _LANES] = lo.astype(o_ref.dtype)
            o_ref[:, half + s * _LANES:half + (s + 1) * _LANES] = hi.astype(o_ref.dtype)

    _prefetched_tiles(t, pl.num_programs(0), ok_ref, start_tile, wait_tile, consume)

    @pl.when(ok_ref[t] == 0)
    def _():
        o_ref[...] = jnp.zeros_like(o_ref)


def _dispatch(h_slabs, plan, tm, cfg):
    k = cfg.d // 2 // _LANES
    rows = plan.row_tok.shape[0]
    return pl.pallas_call(
        functools.partial(_dispatch_kernel, tm=tm, k=k),
        out_shape=jax.ShapeDtypeStruct((rows, cfg.d), _BF16),
        grid_spec=pltpu.PrefetchScalarGridSpec(
            num_scalar_prefetch=2,
            grid=(rows // tm,),
            in_specs=[pl.BlockSpec(memory_space=pl.ANY)],
            out_specs=pl.BlockSpec((tm, cfg.d), lambda t, tok, ok: (t, 0)),
            scratch_shapes=[pltpu.VMEM((tm * k, _LANES), _U32), pltpu.VMEM((tm * k, _LANES), _U32),
                            pltpu.SemaphoreType.DMA((2,))],
        ),
        compiler_params=_cparams("arbitrary"),
        name="moe_dispatch",
    )(plan.row_tok, plan.tile_ok, h_slabs)


def _gate_up_kernel(te_ref, new_ref, ok_ref, x_ref, wg_ref, wu_ref, o_ref, wgb_ref, wub_ref):
    t = pl.program_id(1)

    @pl.when(new_ref[t] == 1)
    def _():
        wgb_ref[...] = wg_ref[...].astype(_BF16)
        wub_ref[...] = wu_ref[...].astype(_BF16)

    @pl.when(ok_ref[t] == 1)
    def _():
        x = x_ref[...]
        a = jnp.dot(x, wgb_ref[...], preferred_element_type=_F32)
        b = jnp.dot(x, wub_ref[...], preferred_element_type=_F32)
        o_ref[...] = (a * jax.nn.sigmoid(a) * b).astype(o_ref.dtype)

    @pl.when(ok_ref[t] == 0)
    def _():
        o_ref[...] = jnp.zeros_like(o_ref)


def _gate_up(xs, w_gate, w_up, l, plan, tm):
    rows, dh = xs.shape
    d, de = w_gate.shape[2:]
    bn = _pick(de, (512, 256, 128))
    n_tiles = rows // tm
    w_spec = pl.BlockSpec((None, None, d, bn), lambda j, t, te, new, ok: (l, te[t], 0, j))
    return pl.pallas_call(
        _gate_up_kernel,
        out_shape=jax.ShapeDtypeStruct((rows, de), _BF16),
        grid_spec=pltpu.PrefetchScalarGridSpec(
            num_scalar_prefetch=3,
            grid=(de // bn, n_tiles),
            in_specs=[pl.BlockSpec((tm, dh), lambda j, t, te, new, ok: (t, 0)), w_spec, w_spec],
            out_specs=pl.BlockSpec((tm, bn), lambda j, t, te, new, ok: (t, j)),
            scratch_shapes=[pltpu.VMEM((d, bn), _BF16), pltpu.VMEM((d, bn), _BF16)],
        ),
        compiler_params=_cparams("arbitrary", "arbitrary"),
        name="moe_gate_up",
    )(plan.tile_e, plan.tile_new, plan.tile_ok, xs, w_gate, w_up)


def _down_kernel(te_ref, new_ref, ok_ref, x_ref, w_ref, o_ref, wb_ref):
    t = pl.program_id(1)

    @pl.when(new_ref[t] == 1)
    def _():
        wb_ref[...] = w_ref[...].astype(_BF16)

    @pl.when(ok_ref[t] == 1)
    def _():
        y = jnp.dot(x_ref[...], wb_ref[...], preferred_element_type=_F32)
        _store_slabs(o_ref, _pack_bf16_pairs(y))

    @pl.when(ok_ref[t] == 0)
    def _():
        o_ref[...] = jnp.zeros_like(o_ref)


def _down_block(d):
    return _pick(d, (2048, 1024, 512, 256))


def _down(hm, w_down, l, plan, tm):
    rows, de = hm.shape
    d = w_down.shape[3]
    bn = _down_block(d)
    k = bn // 2 // _LANES
    n_tiles = rows // tm
    return pl.pallas_call(
        _down_kernel,
        out_shape=jax.ShapeDtypeStruct((d // bn, rows * k, _LANES), _U32),
        grid_spec=pltpu.PrefetchScalarGridSpec(
            num_scalar_prefetch=3,
            grid=(d // bn, n_tiles),
            in_specs=[
                pl.BlockSpec((tm, de), lambda j, t, te, new, ok: (t, 0)),
                pl.BlockSpec((None, None, de, bn), lambda j, t, te, new, ok: (l, te[t], 0, j)),
            ],
            out_specs=pl.BlockSpec((None, tm * k, _LANES), lambda j, t, te, new, ok: (j, t, 0)),
            scratch_shapes=[pltpu.VMEM((de, bn), _BF16)],
        ),
        compiler_params=_cparams("arbitrary", "arbitrary"),
        name="moe_down",
    )(plan.tile_e, plan.tile_new, plan.tile_ok, hm, w_down)


def _combine_kernel(pos_ref, ok_ref, y_hbm, x_ref, g_ref, w_ref, gn_ref, sc_ref, sh_ref, o_ref, h_ref,
                    buf0, buf1, sem, *, bt, n, bn, final):
    t = pl.program_id(0)
    bufs = (buf0, buf1)
    nj = y_hbm.shape[0]
    k = bn // 2 // _LANES

    def slab_copy(slot, c, j, r, p):
        return pltpu.make_async_copy(y_hbm.at[j, pl.ds(p * k, k)], bufs[slot].at[c, j, pl.ds(r * k, k)],
                                     sem.at[slot])

    def start_tile(tile, slot):
        base = tile * bt

        def body(r, carry):
            for c in range(2):
                p = pos_ref[c * n + base + r]
                for j in range(nj):
                    slab_copy(slot, c, j, r, p).start(priority=(c * nj + j) % _DMA_QUEUES)
            return carry

        lax.fori_loop(0, bt, body, 0, unroll=_DMA_UNROLL // 2)

    def wait_tile(slot):
        def body(r, carry):
            for c in range(2):
                for j in range(nj):
                    slab_copy(slot, c, j, r, 0).wait()
            return carry

        lax.fori_loop(0, bt, body, 0, unroll=_DMA_UNROLL // 2)

    def consume(slot):
        w0, w1 = w_ref[:, 0:1], w_ref[:, 1:2]
        g = g_ref[0]
        for j in range(nj):
            for s in range(k):
                lo0, hi0 = _unpack_bf16_pairs(_load_slab_col(bufs[slot].at[0, j], s, bt, k))
                lo1, hi1 = _unpack_bf16_pairs(_load_slab_col(bufs[slot].at[1, j], s, bt, k))
                for c0, y in ((j * bn + s * _LANES, w0 * lo0 + w1 * lo1),
                              (j * bn + bn // 2 + s * _LANES, w0 * hi0 + w1 * hi1)):
                    o_ref[:, c0:c0 + _LANES] = x_ref[:, c0:c0 + _LANES] + g[:, c0:c0 + _LANES] * y
        xn = _rms(o_ref[...], gn_ref[...])
        if final:
            h_ref[...] = xn
        else:
            h_ref[...] = (xn * (1.0 + sc_ref[0]) + sh_ref[0]).astype(h_ref.dtype)

    _prefetched_tiles(t, pl.num_programs(0), ok_ref, start_tile, wait_tile, consume)


def _combine(y, pos, wts_t, x, mod, l, g_next, l_next, cfg, *, final):
    n, d = x.shape
    bt = _row_tile(cfg, (256, 128))
    bn = _down_block(d)
    nj, _, lanes = y.shape
    n_t = n // bt
    buf = pltpu.VMEM((2, nj, bt * (bn // 2 // lanes), lanes), _U32)
    grp = lambda i, pos, ok: _group_of_tile(i, bt, cfg)
    row = pl.BlockSpec((bt, d), lambda i, pos, ok: (i, 0))
    return pl.pallas_call(
        functools.partial(_combine_kernel, bt=bt, n=n, bn=bn, final=final),
        out_shape=(jax.ShapeDtypeStruct((n, d), _F32), jax.ShapeDtypeStruct((n, d), _F32 if final else _BF16)),
        grid_spec=pltpu.PrefetchScalarGridSpec(
            num_scalar_prefetch=2,
            grid=(n_t,),
            in_specs=[
                pl.BlockSpec(memory_space=pl.ANY),
                row,
                _mod_spec(l, _GATE2, d, grp),
                pl.BlockSpec((bt, 2), lambda i, pos, ok: (i, 0)),
                _layer_spec(g_next, l_next),
                _mod_spec(l_next, _SCALE1, d, grp),
                _mod_spec(l_next, _SHIFT1, d, grp),
            ],
            out_specs=(row, row),
            scratch_shapes=[buf, buf, pltpu.SemaphoreType.DMA((2,))],
        ),
        compiler_params=_cparams("arbitrary"),
        name="moe_combine",
    )(pos, jnp.ones((n_t,), jnp.int32), y, x, mod, wts_t, g_next, mod, mod)


def _rope_tables(cfg):
    t = cfg.s_s
    pos = jnp.arange(t, dtype=jnp.int32)
    row = (pos // _GRID_W).astype(_F32)
    col = (pos % _GRID_W).astype(_F32)

    def angles(rot_dim):
        quarter = rot_dim // 4
        inv_freq = _ROPE_THETA ** (-jnp.arange(quarter, dtype=_F32) / quarter)
        return jnp.concatenate([row[:, None] * inv_freq, col[:, None] * inv_freq], axis=-1)

    def stack(prompt_val, lat):
        lat = jnp.tile(lat, (cfg.b_s, 1))
        return jnp.concatenate([jnp.full((cfg.n_p, lat.shape[1]), prompt_val, _F32), lat], axis=0)

    ang_a, ang_b = angles(cfg.rope_a), angles(cfg.hd_b)
    cos_a, sin_a = jnp.cos(ang_a), jnp.sin(ang_a)
    cos_b, sin_b = jnp.cos(ang_b), jnp.sin(ang_b)
    return (stack(1.0, jnp.concatenate([cos_a, cos_a], -1)), stack(0.0, jnp.concatenate([sin_a, sin_a], -1)),
            stack(1.0, jnp.concatenate([cos_b, cos_b], -1)), stack(0.0, jnp.concatenate([-sin_b, sin_b], -1)))


def _keys_with_cache(own, cache, cfg):
    parts = []
    for b in range(cfg.b_s):
        parts += [cache[b].astype(own.dtype), own[cfg.n_p + b * cfg.s_s:cfg.n_p + (b + 1) * cfg.s_s]]
    parts.append(own[:cfg.n_p])
    return jnp.concatenate(parts, axis=0)


def _make_cfg(x_prompt, x_sample, cache_mla_ckv, cache_mla_krope, cache_gqa_k, w_ada, w_in, g_q_lat,
              w_uq, w_ukv, w_dw, w_router, w_gate):
    b, s, d = x_prompt.shape
    b_s, s_s, _ = x_sample.shape
    past, kv_lora = cache_mla_ckv.shape[2:]
    rope_a = cache_mla_krope.shape[3]
    kv_b, hd_b = cache_gqa_k.shape[3:]
    q_lora = g_q_lat.shape[1]
    conv_w, c_conv = w_dw.shape[1:]
    h_b = (w_in.shape[2] - q_lora - kv_lora - rope_a - 2 * kv_b * hd_b - 2 * c_conv) // hd_b
    hv = d - h_b * hd_b - c_conv
    h_a = (w_uq.shape[2] - w_ukv.shape[2] + hv) // rope_a
    v_a = hv // h_a
    nope_a = w_ukv.shape[2] // h_a - v_a
    return _Cfg(d=d, n_p=b * s, s_p=s, b_s=b_s, s_s=s_s, past=past, depth=w_ada.shape[0], q_lora=q_lora,
                kv_lora=kv_lora, rope_a=rope_a, nope_a=nope_a, v_a=v_a, h_a=h_a, h_b=h_b, kv_b=kv_b,
                hd_b=hd_b, c_conv=c_conv, conv_w=conv_w, n_exp=w_router.shape[1], d_exp=w_gate.shape[3])


def kernel(x_prompt, x_sample, c, c_ctx, cache_mla_ckv, cache_mla_krope, cache_gqa_k, cache_gqa_v, w_ada, b_ada, g_norm1, g_norm2, w_in, g_q_lat, g_kv_lat, w_uq, w_ukv, g_q_b, g_k_b, w_dw, b_dw, g_conv_ln, b_conv_ln, w_out, w_router, router_bias, w_gate, w_up, w_down, g_final):
    cfg = _make_cfg(x_prompt, x_sample, cache_mla_ckv, cache_mla_krope, cache_gqa_k, w_ada, w_in, g_q_lat,
                    w_uq, w_ukv, w_dw, w_router, w_gate)
    d, n_p, depth = cfg.d, cfg.n_p, cfg.depth
    batch, seq = x_prompt.shape[:2]
    kvw = cfg.kv_b * cfg.hd_b
    assert cfg.v_a == cfg.hd_b and 1 + cfg.b_s <= _MOD_ROWS

    x = jnp.concatenate([x_prompt.reshape(n_p, d), x_sample.reshape(cfg.n_s, d)], axis=0)

    m = jnp.concatenate([c_ctx[None, :], c, jnp.zeros((_MOD_ROWS - 1 - cfg.b_s, d), _F32)], axis=0)
    mod = _ada(jax.nn.silu(m).astype(_BF16), w_ada, b_ada).reshape(depth, _MOD_ROWS, 6, 1, d)

    vec = lambda a: a.reshape(depth, 1, a.shape[-1])
    g_norm1, g_norm2 = vec(g_norm1), vec(g_norm2)
    gains = [vec(g_q_lat), vec(g_kv_lat), vec(g_q_b), vec(g_k_b)]
    b_dw, g_conv_ln, b_conv_ln = vec(b_dw), vec(g_conv_ln), vec(b_conv_ln)
    w_in_r = _reorder_w_in(w_in, cfg)
    w_uq_r = _reorder_w_uq(w_uq, cfg)
    w_ukv_b = w_ukv.astype(_BF16)
    tabs = _rope_tables(cfg)
    tm = _pick(2 * cfg.n, (512, 256, 128))
    bq_s = _pick(cfg.s_s, (256, 128))

    ckvs, kpes, kbs, vbs = [], [], [], []
    h = _norm_mod(x, g_norm1, mod, 0, cfg)
    for l in range(depth):
        p =_matmul(h, w_in_r, l, _F32, "in_proj")
        q_a, ckv, kpe, q_b, k_b, v_b, k_b16, v_b16, glu = _post_projection(p, w_uq_r, gains, l, tabs, cfg)
        ckvs.append(ckv[:n_p])
        kpes.append(kpe[:n_p])
        kbs.append(k_b[:n_p])
        vbs.append(v_b[:n_p])

        k_a, v_a = _kv_a(_keys_with_cache(ckv, cache_mla_ckv[:, l], cfg),
                         _keys_with_cache(kpe, cache_mla_krope[:, l], cfg), w_ukv_b, l, cfg)
        common_a = dict(group=1, dv=cfg.v_a, kv_head_major=True)
        a_p = _attention(q_a, k_a, v_a, n_seq=batch, tq=seq, tk=seq, q_row0=0, k_row0=cfg.b_s * cfg.tk_s,
                         hb=cfg.h_a, bq=seq, name="attn_a_prompt", **common_a)
        a_s = _attention(q_a, k_a, v_a, n_seq=cfg.b_s, tq=cfg.s_s, tk=cfg.tk_s, q_row0=n_p, k_row0=0,
                         hb=2 if cfg.h_a % 2 == 0 else 1, bq=bq_s, name="attn_a_latent", **common_a)
        k_bk = _keys_with_cache(k_b16, cache_gqa_k[:, l].reshape(cfg.b_s, cfg.past, kvw), cfg)
        v_bk = _keys_with_cache(v_b16, cache_gqa_v[:, l].reshape(cfg.b_s, cfg.past, kvw), cfg)
        grp = cfg.h_b // cfg.kv_b
        common_b = dict(group=grp, dv=cfg.hd_b, kv_head_major=False)
        b_p = _attention(q_b, k_bk, v_bk, n_seq=batch, tq=seq, tk=seq, q_row0=0, k_row0=cfg.b_s * cfg.tk_s,
                         hb=cfg.h_b, bq=seq, name="attn_b_prompt", **common_b)
        b_s = _attention(q_b, k_bk, v_bk, n_seq=cfg.b_s, tq=cfg.s_s, tk=cfg.tk_s, q_row0=n_p, k_row0=0,
                         hb=grp, bq=bq_s, name="attn_b_latent", **common_b)
        out_c = _conv_module(glu, w_dw, b_dw, g_conv_ln, b_conv_ln, l, cfg)

        x = _out_proj(jnp.concatenate([a_p, a_s], axis=0), jnp.concatenate([b_p, b_s], axis=0), out_c,
                      w_out, x, mod, l, cfg)

        h2, idx, wts = _norm_router(x, g_norm2, mod, l, w_router, router_bias, cfg)
        plan, n_tiles = _route_plan(idx, cfg.n_exp, tm)
        xs = _dispatch(h2, plan, tm, cfg)
        hm = _gate_up(xs, w_gate, w_up, l, plan, tm)
        y = _down(hm, w_down, l, plan, tm)
        if l + 1 < depth:
            x, h = _combine(y, plan.pos, wts.T, x, mod, l, g_norm1, l + 1, cfg, final=False)
        else:
            x, y_all = _combine(y, plan.pos, wts.T, x, mod, l, g_final.reshape(1, 1, d), 0, cfg, final=True)

    stack = lambda parts, tail: jnp.stack([a.reshape((batch, seq) + tail) for a in parts], axis=1)
    return (y_all[:n_p].reshape(batch, seq, d),
            y_all[n_p:].reshape(cfg.b_s, cfg.s_s, d),
            stack(ckvs, (cfg.kv_lora,)),
            stack(kpes, (cfg.rope_a,)),
            stack(kbs, (cfg.kv_b, cfg.hd_b)),
            stack(vbs, (cfg.kv_b, cfg.hd_b)))
```
